```python
import math
import jax, jax.numpy as jnp
from jax import lax
import numpy as np

D_MODEL = 1024
BATCH = 4
SEQ = 8192
DEPTH = 2
DEC_BATCH = 16
DEC_SEQ = 16
PAST_LEN = 1024

CHUNK = 64
D_MIX = D_MODEL
LRU_WIDTH = D_MIX // 4
LRU_HEADS = 4
LRU_HEAD_DIM = LRU_WIDTH // LRU_HEADS
LRU_CONV = 4
LRU_C = 8.0
RWKV_WIDTH = D_MIX // 2
RWKV_HEAD = 64
RWKV_HEADS = RWKV_WIDTH // RWKV_HEAD
RWKV_DECAY_LORA = 64
RWKV_A_LORA = 64
RWKV_GATE_LORA = 128
RWKV_COLS = 3 * RWKV_WIDTH + RWKV_DECAY_LORA + RWKV_A_LORA + RWKV_GATE_LORA
RWKV_SPLITS = (RWKV_WIDTH, 2 * RWKV_WIDTH, 3 * RWKV_WIDTH,
               3 * RWKV_WIDTH + RWKV_DECAY_LORA,
               3 * RWKV_WIDTH + RWKV_DECAY_LORA + RWKV_A_LORA)
S5_WIDTH = D_MIX - LRU_WIDTH - RWKV_WIDTH
S5_GROUP_DIM = 16
S5_GROUPS = S5_WIDTH // S5_GROUP_DIM
S5_STATE = 64
IN_COLS = 2 * LRU_WIDTH + RWKV_COLS + S5_WIDTH
IN_SPLITS = (LRU_WIDTH, 2 * LRU_WIDTH, 2 * LRU_WIDTH + RWKV_COLS)
D_FF = 2816
FFN_CONV = 3
NORM_EPS = 1e-6
RWKV_GN_EPS = 64e-5

kernel_name = 'hymba_rglru_rwkv7_s5_stream_step'


def _rmsnorm(x, g):
    xf = x.astype(jnp.float32)
    y = xf * lax.rsqrt(jnp.mean(xf * xf, axis=-1, keepdims=True) + NORM_EPS)
    return (y * g.astype(jnp.float32)).astype(x.dtype)


def _causal_dwconv(x, buf, w, b):
    width = w.shape[0]
    t = x.shape[1]
    xp = jnp.concatenate([buf.astype(x.dtype), x], axis=1)
    y = b + xp[:, 0:t] * w[0]
    for k in range(1, width):
        y = y + xp[:, k:k + t] * w[k]
    return y, xp[:, t:]


def _lin_combine(e1, e2):
    a1, b1 = e1
    a2, b2 = e2
    return a1 * a2, a2 * b1 + b2


def _complex_combine(e1, e2):
    ar1, ai1, br1, bi1 = e1
    ar2, ai2, br2, bi2 = e2
    return (ar2 * ar1 - ai2 * ai1,
            ar2 * ai1 + ai2 * ar1,
            ar2 * br1 - ai2 * bi1 + br2,
            ar2 * bi1 + ai2 * br1 + bi2)


def _rglru(u_gate, u_x, buf, h0, conv_w, conv_b, wa, ba, wx, bx, lam):
    bsz, t, _ = u_x.shape
    xc, new_buf = _causal_dwconv(u_x, buf, conv_w, conv_b)
    xf = xc.astype(jnp.float32)
    xh = xf.reshape(bsz, t, LRU_HEADS, LRU_HEAD_DIM)
    r = jax.nn.sigmoid(jnp.einsum('bthi,hij->bthj', xh, wa).reshape(bsz, t, LRU_WIDTH) + ba)
    i = jax.nn.sigmoid(jnp.einsum('bthi,hij->bthj', xh, wx).reshape(bsz, t, LRU_WIDTH) + bx)
    log_a = -LRU_C * r * jax.nn.softplus(-lam.astype(jnp.float32))
    a = jnp.exp(log_a)
    gain = jnp.sqrt(jnp.maximum(-jnp.expm1(2.0 * log_a), 0.0))
    a_cum, b_cum = lax.associative_scan(_lin_combine, (a, gain * i * xf), axis=1)
    h = a_cum * h0.astype(jnp.float32)[:, None] + b_cum
    y = h * jax.nn.gelu(u_gate.astype(jnp.float32))
    return y.astype(u_x.dtype), new_buf, h[:, -1]


def _rwkv7(p_b, shift0, s0, mu, w0, w2, a0, a2, g2, k_k, k_a, r_k, ln_w, ln_b):
    bsz, t, _ = p_b.shape
    pf = p_b.astype(jnp.float32)
    prev = jnp.concatenate([shift0.astype(jnp.float32)[:, None], pf[:, :-1]], axis=1)
    xm = pf + (prev - pf) * mu
    r, k, v, w_lo, a_lo, g_lo = jnp.split(xm, RWKV_SPLITS, axis=-1)
    log_w = -jax.nn.softplus(-(w0 + jnp.tanh(w_lo) @ w2)) - 0.5
    decay = jnp.exp(-jnp.exp(log_w))
    a = jax.nn.sigmoid(a0 + a_lo @ a2)
    g = jax.nn.sigmoid(g_lo) @ g2

    def heads(z):
        return z.reshape(bsz, t, RWKV_HEADS, RWKV_HEAD)

    kk = heads(k * k_k)
    kk = kk * lax.rsqrt(jnp.maximum(jnp.sum(kk * kk, axis=-1, keepdims=True), 1e-24))
    k = k * (1.0 + (a - 1.0) * k_a)
    rh, wh, kh, vh, ah = heads(r), heads(decay), heads(k), heads(v), heads(a)

    def step(s, inp):
        r_t, w_t, k_t, v_t, kk_t, a_t = inp
        sa = jnp.einsum('bhvk,bhk->bhv', s, kk_t)
        s = (s * w_t[:, :, None, :]
             - sa[..., None] * (kk_t * a_t)[:, :, None, :]
             + v_t[..., None] * k_t[:, :, None, :])
        return s, jnp.einsum('bhvk,bhk->bhv', s, r_t)

    xs = tuple(jnp.moveaxis(z, 1, 0) for z in (rh, wh, kh, vh, kk, ah))
    s_last, y = lax.scan(step, s0.astype(jnp.float32), xs)
    y = jnp.moveaxis(y, 0, 1)
    mean = jnp.mean(y, axis=-1, keepdims=True)
    var = jnp.mean(jnp.square(y - mean), axis=-1, keepdims=True)
    yn = ((y - mean) * lax.rsqrt(var + RWKV_GN_EPS)).reshape(bsz, t, RWKV_WIDTH) * ln_w + ln_b
    bonus = (jnp.sum(rh * kh * r_k, axis=-1, keepdims=True) * vh).reshape(bsz, t, RWKV_WIDTH)
    out = (yn + bonus) * g
    return out.astype(p_b.dtype), p_b[:, -1], s_last


def _s5(u, s_re0, s_im0, a_re, a_im, b_re, b_im, c_re, c_im, d, log_dt, glu_w, glu_b):
    bsz, t, _ = u.shape
    uf = u.astype(jnp.float32)
    ug = uf.reshape(bsz, t, S5_GROUPS, S5_GROUP_DIM)
    a_re = a_re.astype(jnp.float32)
    a_im = a_im.astype(jnp.float32)
    dt = jnp.exp(log_dt.astype(jnp.float32))[:, None]
    mag = jnp.exp(dt * a_re)
    abar_re = mag * jnp.cos(dt * a_im)
    abar_im = mag * jnp.sin(dt * a_im)
    den = a_re * a_re + a_im * a_im
    fr = ((abar_re - 1.0) * a_re + abar_im * a_im) / den
    fi = (abar_im * a_re - (abar_re - 1.0) * a_im) / den
    bbar_re = fr[..., None] * b_re - fi[..., None] * b_im
    bbar_im = fr[..., None] * b_im + fi[..., None] * b_re
    bu_re = jnp.einsum('btgi,gpi->btgp', ug, bbar_re)
    bu_im = jnp.einsum('btgi,gpi->btgp', ug, bbar_im)
    ar = jnp.broadcast_to(abar_re, bu_re.shape)
    ai = jnp.broadcast_to(abar_im, bu_im.shape)
    acr, aci, xr, xi = lax.associative_scan(_complex_combine, (ar, ai, bu_re, bu_im), axis=1)
    h0r = s_re0.astype(jnp.float32)[:, None]
    h0i = s_im0.astype(jnp.float32)[:, None]
    h_re = acr * h0r - aci * h0i + xr
    h_im = acr * h0i + aci * h0r + xi
    y = jnp.einsum('btgp,gop->btgo', h_re, c_re) - jnp.einsum('btgp,gop->btgo', h_im, c_im)
    y = y.reshape(bsz, t, S5_WIDTH) + d * uf
    z = jax.nn.gelu(y)
    out = z * jax.nn.sigmoid(z @ glu_w + glu_b)
    return out.astype(u.dtype), h_re[:, -1], h_im[:, -1]


def _conv_ffn(h, buf, w_up, conv_w, conv_b, w_down):
    up = h @ w_up
    up, new_buf = _causal_dwconv(up, buf, conv_w, conv_b)
    val, gate = jnp.split(up, 2, axis=-1)
    return (val * jax.nn.silu(gate)) @ w_down, new_buf


def _layer(x, c, st, p, l):
    lru_buf, lru_h, rw_shift, rw_s, s5_re, s5_im, ffn_buf = st
    mod = jax.nn.silu(c) @ p['w_ada'][l] + p['b_ada'][l]
    sh1, sc1, g1, sh2, sc2, g2 = jnp.split(mod[:, None], 6, axis=-1)
    h = _rmsnorm(x, p['norm_mix'][l]) * (1.0 + sc1) + sh1
    proj = h @ p['w_in'][l]
    u_gate, u_lru, p_rwkv, u_s5 = jnp.split(proj, IN_SPLITS, axis=-1)
    y_a, lru_buf, lru_h = _rglru(u_gate, u_lru, lru_buf, lru_h, p['lru_conv_w'][l], p['lru_conv_b'][l],
                                 p['lru_wa'][l], p['lru_ba'][l], p['lru_wx'][l], p['lru_bx'][l],
                                 p['lru_lambda'][l])
    y_b, rw_shift, rw_s = _rwkv7(p_rwkv, rw_shift, rw_s, p['rwkv_mu'][l], p['rwkv_w0'][l], p['rwkv_w2'][l],
                                 p['rwkv_a0'][l], p['rwkv_a2'][l], p['rwkv_g2'][l], p['rwkv_k_k'][l],
                                 p['rwkv_k_a'][l], p['rwkv_r_k'][l], p['rwkv_ln_w'][l], p['rwkv_ln_b'][l])
    y_c, s5_re, s5_im = _s5(u_s5, s5_re, s5_im, p['s5_a_re'][l], p['s5_a_im'][l], p['s5_b_re'][l],
                            p['s5_b_im'][l], p['s5_c_re'][l], p['s5_c_im'][l], p['s5_d'][l],
                            p['s5_log_dt'][l], p['s5_glu_w'][l], p['s5_glu_b'][l])
    mix = jnp.concatenate([y_a, y_b, y_c], axis=-1) @ p['w_out'][l]
    x = x + g1 * mix
    h2 = _rmsnorm(x, p['norm_ffn'][l]) * (1.0 + sc2) + sh2
    f, ffn_buf = _conv_ffn(h2, ffn_buf, p['ffn_up'][l], p['ffn_conv_w'][l], p['ffn_conv_b'][l], p['ffn_down'][l])
    x = x + g2 * f
    return x, (lru_buf, lru_h, rw_shift, rw_s, s5_re, s5_im, ffn_buf)


def _trunk(x, c, states, p):
    new = []
    for l in range(DEPTH):
        x, st = _layer(x, c, tuple(s[l] for s in states), p, l)
        new.append(st)
    stacked = tuple(jnp.stack([n[i] for n in new], axis=0) for i in range(len(states)))
    return _rmsnorm(x, p['norm_final']), stacked


def _zero_states(bsz, dtype):
    return (jnp.zeros((DEPTH, bsz, LRU_CONV - 1, LRU_WIDTH), dtype),
            jnp.zeros((DEPTH, bsz, LRU_WIDTH), dtype),
            jnp.zeros((DEPTH, bsz, RWKV_COLS), dtype),
            jnp.zeros((DEPTH, bsz, RWKV_HEADS, RWKV_HEAD, RWKV_HEAD), dtype),
            jnp.zeros((DEPTH, bsz, S5_GROUPS, S5_STATE), dtype),
            jnp.zeros((DEPTH, bsz, S5_GROUPS, S5_STATE), dtype),
            jnp.zeros((DEPTH, bsz, FFN_CONV - 1, 2 * D_FF), dtype))


def setup_inputs(seed: int = 0) -> dict:
    key = jax.random.key(seed)
    keys = jax.random.split(key, 64)
    counter = [0]

    def nxt():
        counter[0] += 1
        return keys[counter[0] - 1]

    def nrm(shape, scale):
        return scale * jax.random.normal(nxt(), shape, jnp.float32)

    def uni(shape, lo, hi):
        return jax.random.uniform(nxt(), shape, jnp.float32, lo, hi)

    L = DEPTH
    lru_u = uni((L, LRU_WIDTH), 0.9, 0.999)
    lru_a = jnp.exp(jnp.log(lru_u) / LRU_C)
    lru_lambda = jnp.log(lru_a) - jnp.log1p(-lru_a)
    n_idx = jnp.arange(S5_STATE, dtype=jnp.float32)
    return {
        'x_prompt': nrm((BATCH, SEQ, D_MODEL), 1.0),
        'x_sample': nrm((DEC_BATCH, DEC_SEQ, D_MODEL), 1.0),
        'c_prompt': nrm((BATCH, D_MODEL), 1.0),
        'c_sample': nrm((DEC_BATCH, D_MODEL), 1.0),
        'state_lru_conv': nrm((L, DEC_BATCH, LRU_CONV - 1, LRU_WIDTH), 1.0),
        'state_lru_h': nrm((L, DEC_BATCH, LRU_WIDTH), 0.5),
        'state_rwkv_shift': nrm((L, DEC_BATCH, RWKV_COLS), 1.0),
        'state_rwkv_S': nrm((L, DEC_BATCH, RWKV_HEADS, RWKV_HEAD, RWKV_HEAD), 0.5),
        'state_s5_re': nrm((L, DEC_BATCH, S5_GROUPS, S5_STATE), 0.5),
        'state_s5_im': nrm((L, DEC_BATCH, S5_GROUPS, S5_STATE), 0.5),
        'state_ffn_conv': nrm((L, DEC_BATCH, FFN_CONV - 1, 2 * D_FF), 1.0),
        'w_ada': nrm((L, D_MODEL, 6 * D_MODEL), 0.5 * D_MODEL ** -0.5),
        'b_ada': nrm((L, 6 * D_MODEL), 0.05),
        'norm_mix': 1.0 + nrm((L, D_MODEL), 0.1),
        'norm_ffn': 1.0 + nrm((L, D_MODEL), 0.1),
        'w_in': nrm((L, D_MODEL, IN_COLS), D_MODEL ** -0.5),
        'w_out': nrm((L, D_MIX, D_MODEL), D_MIX ** -0.5),
        'lru_conv_w': nrm((L, LRU_CONV, LRU_WIDTH), 0.5),
        'lru_conv_b': nrm((L, LRU_WIDTH), 0.05),
        'lru_wa': nrm((L, LRU_HEADS, LRU_HEAD_DIM, LRU_HEAD_DIM), LRU_HEAD_DIM ** -0.5),
        'lru_ba': nrm((L, LRU_WIDTH), 0.1),
        'lru_wx': nrm((L, LRU_HEADS, LRU_HEAD_DIM, LRU_HEAD_DIM), LRU_HEAD_DIM ** -0.5),
        'lru_bx': nrm((L, LRU_WIDTH), 0.1),
        'lru_lambda': lru_lambda,
        'rwkv_mu': uni((L, RWKV_COLS), 0.0, 1.0),
        'rwkv_w0': uni((L, RWKV_WIDTH), -6.5, -1.5),
        'rwkv_w2': nrm((L, RWKV_DECAY_LORA, RWKV_WIDTH), 0.1),
        'rwkv_a0': nrm((L, RWKV_WIDTH), 0.1),
        'rwkv_a2': nrm((L, RWKV_A_LORA, RWKV_WIDTH), 0.5 * RWKV_A_LORA ** -0.5),
        'rwkv_g2': nrm((L, RWKV_GATE_LORA, RWKV_WIDTH), RWKV_GATE_LORA ** -0.5),
        'rwkv_k_k': 0.85 + nrm((L, RWKV_WIDTH), 0.1),
        'rwkv_k_a': 1.0 + nrm((L, RWKV_WIDTH), 0.1),
        'rwkv_r_k': nrm((L, RWKV_HEADS, RWKV_HEAD), 0.1),
        'rwkv_ln_w': 1.0 + nrm((L, RWKV_WIDTH), 0.1),
        'rwkv_ln_b': nrm((L, RWKV_WIDTH), 0.05),
        's5_a_re': -0.5 + nrm((L, S5_GROUPS, S5_STATE), 0.01),
        's5_a_im': jnp.pi * n_idx + nrm((L, S5_GROUPS, S5_STATE), 0.01),
        's5_b_re': nrm((L, S5_GROUPS, S5_STATE, S5_GROUP_DIM), (2.0 * S5_GROUP_DIM) ** -0.5),
        's5_b_im': nrm((L, S5_GROUPS, S5_STATE, S5_GROUP_DIM), (2.0 * S5_GROUP_DIM) ** -0.5),
        's5_c_re': nrm((L, S5_GROUPS, S5_GROUP_DIM, S5_STATE), (2.0 * S5_STATE) ** -0.5),
        's5_c_im': nrm((L, S5_GROUPS, S5_GROUP_DIM, S5_STATE), (2.0 * S5_STATE) ** -0.5),
        's5_d': nrm((L, S5_WIDTH), 0.5),
        's5_log_dt': uni((L, S5_GROUPS), math.log(0.001), math.log(0.1)),
        's5_glu_w': nrm((L, S5_WIDTH, S5_WIDTH), S5_WIDTH ** -0.5),
        's5_glu_b': nrm((L, S5_WIDTH), 0.05),
        'ffn_up': nrm((L, D_MODEL, 2 * D_FF), D_MODEL ** -0.5),
        'ffn_conv_w': nrm((L, FFN_CONV, 2 * D_FF), 0.5),
        'ffn_conv_b': nrm((L, 2 * D_FF), 0.05),
        'ffn_down': nrm((L, D_FF, D_MODEL), D_FF ** -0.5),
        'norm_final': 1.0 + nrm((D_MODEL,), 0.1),
    }


def reference(x_prompt, x_sample, c_prompt, c_sample,
              state_lru_conv, state_lru_h, state_rwkv_shift, state_rwkv_S,
              state_s5_re, state_s5_im, state_ffn_conv,
              w_ada, b_ada, norm_mix, norm_ffn, w_in, w_out,
              lru_conv_w, lru_conv_b, lru_wa, lru_ba, lru_wx, lru_bx, lru_lambda,
              rwkv_mu, rwkv_w0, rwkv_w2, rwkv_a0, rwkv_a2, rwkv_g2, rwkv_k_k, rwkv_k_a, rwkv_r_k,
              rwkv_ln_w, rwkv_ln_b,
              s5_a_re, s5_a_im, s5_b_re, s5_b_im, s5_c_re, s5_c_im, s5_d, s5_log_dt, s5_glu_w, s5_glu_b,
              ffn_up, ffn_conv_w, ffn_conv_b, ffn_down, norm_final):
    p = dict(w_ada=w_ada, b_ada=b_ada, norm_mix=norm_mix, norm_ffn=norm_ffn, w_in=w_in, w_out=w_out,
             lru_conv_w=lru_conv_w, lru_conv_b=lru_conv_b, lru_wa=lru_wa, lru_ba=lru_ba,
             lru_wx=lru_wx, lru_bx=lru_bx, lru_lambda=lru_lambda,
             rwkv_mu=rwkv_mu, rwkv_w0=rwkv_w0, rwkv_w2=rwkv_w2, rwkv_a0=rwkv_a0, rwkv_a2=rwkv_a2,
             rwkv_g2=rwkv_g2, rwkv_k_k=rwkv_k_k, rwkv_k_a=rwkv_k_a, rwkv_r_k=rwkv_r_k,
             rwkv_ln_w=rwkv_ln_w, rwkv_ln_b=rwkv_ln_b,
             s5_a_re=s5_a_re, s5_a_im=s5_a_im, s5_b_re=s5_b_re, s5_b_im=s5_b_im,
             s5_c_re=s5_c_re, s5_c_im=s5_c_im, s5_d=s5_d, s5_log_dt=s5_log_dt,
             s5_glu_w=s5_glu_w, s5_glu_b=s5_glu_b,
             ffn_up=ffn_up, ffn_conv_w=ffn_conv_w, ffn_conv_b=ffn_conv_b, ffn_down=ffn_down,
             norm_final=norm_final)
    y_prompt, p_states = _trunk(x_prompt, c_prompt, _zero_states(x_prompt.shape[0], x_prompt.dtype), p)
    s_in = (state_lru_conv, state_lru_h, state_rwkv_shift, state_rwkv_S, state_s5_re, state_s5_im, state_ffn_conv)
    y_sample, s_states = _trunk(x_sample, c_sample, s_in, p)
    p_lru_conv, p_lru_h, p_rwkv_shift, p_rwkv_S, p_s5_re, p_s5_im, p_ffn_conv = p_states
    s_lru_conv, s_lru_h, s_rwkv_shift, s_rwkv_S, s_s5_re, s_s5_im, s_ffn_conv = s_states
    return (y_prompt, y_sample,
            p_lru_conv, p_lru_h, p_rwkv_shift, p_rwkv_S, p_s5_re, p_s5_im, p_ffn_conv,
            s_lru_conv, s_lru_h, s_rwkv_shift, s_rwkv_S, s_s5_re, s_s5_im, s_ffn_conv)
```

```python
import functools
import math
from typing import NamedTuple

import jax
import jax.numpy as jnp
from jax import lax
from jax.experimental import pallas as pl
from jax.experimental.pallas import tpu as pltpu

F32 = jnp.float32
BF16 = jnp.bfloat16
HIGHEST = lax.Precision.HIGHEST

D_MODEL = 1024
LRU_WIDTH = 256
LRU_CONV = 4
LRU_C = 8.0
RWKV_WIDTH = 512
RWKV_HEAD = 64
RWKV_HEADS = 8
RWKV_COLS = 1792
S5_WIDTH = 256
S5_STATES = 1024
IN_COLS = 2560
D_FF = 2816
NORM_EPS = 1e-6
RWKV_GN_EPS = 64e-5
SUBLANES = 8
FFN_COL_BLOCK = 256
VMEM_LIMIT_BYTES = 56 * 1024 * 1024


class Tiling(NamedTuple):
    t_mix: int
    chunk: int
    group: int
    t_ffn: int


def _tiling(seq_len):
    t_mix = min(seq_len, 256)
    chunk = min(t_mix, 64)
    group = min(RWKV_HEADS, 256 // chunk)
    return Tiling(t_mix, chunk, group, min(seq_len, 512))


def _dg(a, b, ca, cb, precision=None):
    return lax.dot_general(a, b, (((ca,), (cb,)), ((), ())), precision=precision,
                           preferred_element_type=F32)


def _mm(a, w):
    return jnp.dot(a.astype(BF16), w, preferred_element_type=F32)


def _expm1(x):
    u = jnp.exp(x)
    small = u >= 0.5
    stable = (u - 1.0) * x / jnp.log(jnp.where(small, u, 0.75))
    return jnp.where(u == 1.0, x, jnp.where(small, stable, u - 1.0))


def _shift_rows(x, d, fill):
    rolled = pltpu.roll(x, d, axis=0)
    row = lax.broadcasted_iota(jnp.int32, x.shape, 0)
    return jnp.where(row >= d, rolled, fill)


def _linear_scan(a, b):
    d = 1
    while d < a.shape[0]:
        b = a * _shift_rows(b, d, 0.0) + b
        a = a * _shift_rows(a, d, 1.0)
        d *= 2
    return b


def _block_diag(x, group):
    rows, cols = x.shape
    width = cols // group
    tiled = jnp.concatenate([x] * group, axis=0)
    rb = lax.broadcasted_iota(jnp.int32, tiled.shape, 0) >> int(math.log2(rows))
    cb = lax.broadcasted_iota(jnp.int32, tiled.shape, 1) >> int(math.log2(width))
    return jnp.where(rb == cb, tiled, 0.0)


def _fold_diag(m, group):
    rows = m.shape[0] // group
    width = m.shape[1] // group
    rb = lax.broadcasted_iota(jnp.int32, m.shape, 0) >> int(math.log2(rows))
    cb = lax.broadcasted_iota(jnp.int32, m.shape, 1) >> int(math.log2(width))
    m = jnp.where(rb == cb, m, 0.0)
    out = m[0:rows]
    for h in range(1, group):
        out = out + m[h * rows:(h + 1) * rows]
    return out


def _rmsnorm_mod(x, gain, scale, shift):
    y = x * lax.rsqrt(jnp.mean(x * x, axis=-1, keepdims=True) + NORM_EPS)
    return (y * gain) * (1.0 + scale) + shift


def _ada_kernel(c_ref, w_ref, b_ref, o_ref):
    c = c_ref[...]
    o_ref[...] = _mm(c * jax.nn.sigmoid(c), w_ref[...].astype(BF16)) + b_ref[...]


def _ada(c, w_ada, b_ada):
    depth, _, n = w_ada.shape
    rows = c.shape[0]
    tn = 1536
    return pl.pallas_call(
        _ada_kernel,
        grid=(depth, n // tn),
        in_specs=[pl.BlockSpec((rows, D_MODEL), lambda l, j: (0, 0)),
                  pl.BlockSpec((None, D_MODEL, tn), lambda l, j: (l, 0, j)),
                  pl.BlockSpec((None, 1, tn), lambda l, j: (l, 0, j))],
        out_specs=pl.BlockSpec((None, rows, tn), lambda l, j: (l, 0, j)),
        out_shape=jax.ShapeDtypeStruct((depth, rows, n), F32),
        compiler_params=pltpu.CompilerParams(
            dimension_semantics=("arbitrary", "arbitrary"), vmem_limit_bytes=VMEM_LIMIT_BYTES),
        name="ada_mod",
    )(c, w_ada, b_ada.reshape(depth, 1, n))


def _mixer_kernel(tl,
                  x_ref, mod_ref, norm_ref, w_in_ref, w_out_ref,
                  conv_w_ref, conv_b_ref, wa_ref, ba_ref, wx_ref, bx_ref, lam_ref,
                  lru_buf0_ref, lru_h0_ref,
                  mu_ref, w0_ref, w2_ref, a0_ref, a2_ref, g2_ref, kk_ref, ka_ref, rk_ref,
                  lnw_ref, lnb_ref, ones_ref, shift0_ref, s0_ref,
                  are_ref, aim_ref, ldt_ref, bre_ref, bim_ref, cre_ref, cim_ref, d_ref,
                  gluw_ref, glub_ref, s5re0_ref, s5im0_ref,
                  xo_ref, lru_buf_o, lru_h_o, shift_o, s_o, s5re_o, s5im_o,
                  lru_ext, pw_re, pw_im, bbar_re, bbar_im,
                  r_s, k_s, v_s, kk_s, b_s, ld_s, y_s):
    tc, chunk, group = tl.t_mix, tl.chunk, tl.group
    n_scan = int(math.log2(tc))
    i = pl.program_id(1)

    @pl.when(i == 0)
    def _init():
        lru_ext[0:SUBLANES, :] = lru_buf0_ref[...]
        lru_h_o[...] = lru_h0_ref[...]
        shift_o[...] = shift0_ref[...]
        s_o[...] = s0_ref[...]
        s5re_o[...] = s5re0_ref[...]
        s5im_o[...] = s5im0_ref[...]
        dt = jnp.exp(ldt_ref[...])
        a_re = are_ref[...]
        a_im = aim_ref[...]
        mag = jnp.exp(dt * a_re)
        abr = mag * jnp.cos(dt * a_im)
        abi = mag * jnp.sin(dt * a_im)
        den = a_re * a_re + a_im * a_im
        fr = ((abr - 1.0) * a_re + abi * a_im) / den
        fi = (abi * a_re - (abr - 1.0) * a_im) / den
        bbar_re[...] = (fr * bre_ref[...] - fi * bim_ref[...]).astype(BF16)
        bbar_im[...] = (fr * bim_ref[...] + fi * bre_ref[...]).astype(BF16)
        pr, pi = abr, abi
        for k in range(max(n_scan, 1)):
            pw_re[k:k + 1, :] = pr
            pw_im[k:k + 1, :] = pi
            pr, pi = pr * pr - pi * pi, 2.0 * pr * pi

    x = x_ref[...]
    row = lax.broadcasted_iota(jnp.int32, (tc, 1), 0)
    h = _rmsnorm_mod(x, norm_ref[...], mod_ref[1:2, :], mod_ref[0:1, :])
    proj = _mm(h, w_in_ref[...])
    u_gate = proj[:, 0:256]
    u_lru = proj[:, 256:512]
    p_rw = proj[:, 512:2304]
    u_s5 = proj[:, 2304:2560]

    lru_ext[SUBLANES:SUBLANES + tc, :] = u_lru
    cw = conv_w_ref[...]
    base = SUBLANES - (LRU_CONV - 1)
    xc = conv_b_ref[...] + lru_ext[base:base + tc, :] * cw[0:1]
    for k in range(1, LRU_CONV):
        xc = xc + lru_ext[base + k:base + k + tc, :] * cw[k:k + 1]
    tail = lru_ext[tc:tc + SUBLANES, :]
    lru_buf_o[...] = tail
    lru_ext[0:SUBLANES, :] = tail
    r_gate = jax.nn.sigmoid(_mm(xc, wa_ref[...]) + ba_ref[...])
    i_gate = jax.nn.sigmoid(_mm(xc, wx_ref[...]) + bx_ref[...])
    log_a = -LRU_C * r_gate * jax.nn.softplus(-lam_ref[...])
    a = jnp.exp(log_a)
    gain = jnp.sqrt(jnp.maximum(-_expm1(2.0 * log_a), 0.0))
    b = gain * i_gate * xc
    b = b + jnp.where(row == 0, a * lru_h_o[...], 0.0)
    h_lru = _linear_scan(a, b)
    lru_h_o[...] = h_lru[tc - 1:tc, :]
    y_a = h_lru * jax.nn.gelu(u_gate)

    prev = jnp.where(row == 0, shift_o[...], pltpu.roll(p_rw, 1, axis=0))
    shift_o[...] = p_rw[tc - 1:tc, :]
    xm = p_rw + (prev - p_rw) * mu_ref[...]
    r_t = xm[:, 0:512]
    k_t = xm[:, 512:1024]
    v_t = xm[:, 1024:1536]
    lo = xm[:, 1536:1664]
    g_lo = xm[:, 1664:1792]
    ones_bd = ones_ref[...]

    def head_sum(z):
        return _dg(z, ones_bd, 1, 0, HIGHEST)

    log_w = -jax.nn.softplus(-(w0_ref[...] + _mm(jnp.tanh(lo), w2_ref[...]))) - 0.5
    a_t = jax.nn.sigmoid(a0_ref[...] + _mm(lo, a2_ref[...]))
    g_t = _mm(jax.nn.sigmoid(g_lo), g2_ref[...])
    kk = k_t * kk_ref[...]
    kk = kk * lax.rsqrt(jnp.maximum(head_sum(kk * kk), 1e-24))
    kp = k_t * (1.0 + (a_t - 1.0) * ka_ref[...])
    r_s[...] = r_t
    k_s[...] = kp
    v_s[...] = v_t
    kk_s[...] = kk
    b_s[...] = kk * a_t
    ld_s[...] = -jnp.exp(log_w)

    gw = group * RWKV_HEAD
    crow = lax.broadcasted_iota(jnp.int32, (chunk, group * chunk), 0)
    ccol = lax.broadcasted_iota(jnp.int32, (chunk, group * chunk), 1) & (chunk - 1)
    strict = ccol < crow
    incl = ccol <= crow
    n_pow = int(math.log2(chunk))

    def chunk_body(c, carry):
        sl = pl.ds(pl.multiple_of(c * chunk, chunk), chunk)
        r = r_s[sl, :]
        kpc = k_s[sl, :]
        v = v_s[sl, :]
        kkc = kk_s[sl, :]
        bq = b_s[sl, :]
        ld = ld_s[sl, :]
        cum = ld
        d = 1
        while d < chunk:
            cum = cum + _shift_rows(cum, d, 0.0)
            d *= 2
        cum_end = cum[chunk - 1:chunk, :]
        inv = jnp.exp(-cum)
        tail_decay = jnp.exp(cum_end - cum)
        rt = r * jnp.exp(cum)
        kkt = kkc * jnp.exp(cum - ld)
        kb = kpc * inv
        bb = bq * inv
        kg = kpc * tail_decay
        bg = bq * tail_decay
        g_end = jnp.exp(cum_end)
        for g in range(RWKV_HEADS // group):
            ls = slice(g * gw, (g + 1) * gw)
            s_prev = s_o[:, ls]
            bd_kb = _block_diag(kb[:, ls], group)
            bd_bb = _block_diag(bb[:, ls], group)
            bd_v = _block_diag(v[:, ls], group)
            bd_s = _block_diag(s_prev, group)
            kkt_g = kkt[:, ls]
            rt_g = rt[:, ls]
            neg_lb = jnp.where(strict, -_dg(kkt_g, bd_bb, 1, 1, HIGHEST), 0.0)
            lk = jnp.where(strict, _dg(kkt_g, bd_kb, 1, 1, HIGHEST), 0.0)
            rk = jnp.where(incl, _dg(rt_g, bd_kb, 1, 1, HIGHEST), 0.0)
            rb = jnp.where(incl, _dg(rt_g, bd_bb, 1, 1, HIGHEST), 0.0)
            u = _dg(kkt_g, bd_s, 1, 1, HIGHEST) + _dg(lk, bd_v, 1, 0, HIGHEST)
            p = neg_lb
            for q in range(n_pow):
                u = u + _dg(p, _block_diag(u, group), 1, 0, HIGHEST)
                if q + 1 < n_pow:
                    p = _dg(p, _block_diag(p, group), 1, 0, HIGHEST)
            y = (_dg(rt_g, bd_s, 1, 1, HIGHEST) + _dg(rk, bd_v, 1, 0, HIGHEST)
                 - _dg(rb, _block_diag(u, group), 1, 0, HIGHEST))
            y_s[sl, ls] = y
            s_new = (s_prev * g_end[:, ls]
                     + _fold_diag(_dg(v[:, ls], kg[:, ls], 0, 0, HIGHEST), group)
                     - _fold_diag(_dg(u, bg[:, ls], 0, 0, HIGHEST), group))
            s_o[:, ls] = s_new
        return carry

    lax.fori_loop(0, tc // chunk, chunk_body, 0)

    y = y_s[...]
    inv_n = 1.0 / RWKV_HEAD
    mean = head_sum(y) * inv_n
    yc = y - mean
    var = head_sum(yc * yc) * inv_n
    yn = yc * lax.rsqrt(var + RWKV_GN_EPS) * lnw_ref[...] + lnb_ref[...]
    bonus = head_sum(r_t * kp * rk_ref[...]) * v_t
    y_b = (yn + bonus) * g_t

    bu_re = _mm(u_s5, bbar_re[...])
    bu_im = _mm(u_s5, bbar_im[...])
    h0r = s5re_o[...]
    h0i = s5im_o[...]
    ar = pw_re[0:1, :]
    ai = pw_im[0:1, :]
    first = row == 0
    hr = bu_re + jnp.where(first, ar * h0r - ai * h0i, 0.0)
    hi = bu_im + jnp.where(first, ar * h0i + ai * h0r, 0.0)
    d = 1
    for k in range(n_scan):
        pr = pw_re[k:k + 1, :]
        pi = pw_im[k:k + 1, :]
        sr = _shift_rows(hr, d, 0.0)
        si = _shift_rows(hi, d, 0.0)
        hr, hi = hr + (pr * sr - pi * si), hi + (pr * si + pi * sr)
        d *= 2
    s5re_o[...] = hr[tc - 1:tc, :]
    s5im_o[...] = hi[tc - 1:tc, :]
    y5 = _mm(hr, cre_ref[...]) - _mm(hi, cim_ref[...]) + d_ref[...] * u_s5
    z = jax.nn.gelu(y5)
    y_c = z * jax.nn.sigmoid(_mm(z, gluw_ref[...]) + glub_ref[...])

    mix = _mm(jnp.concatenate([y_a, y_b, y_c], axis=-1), w_out_ref[...])
    xo_ref[...] = x + mod_ref[2:3, :] * mix


def _const_spec(arr):
    nd = arr.ndim
    return pl.BlockSpec(arr.shape, lambda b, i: (0,) * nd)


def _batch_spec(arr):
    nd = arr.ndim
    return pl.BlockSpec((None,) + arr.shape[1:], lambda b, i: (b,) + (0,) * (nd - 1))


def _mixer(tl, x, mod, lp, st):
    bsz, seq, _ = x.shape
    tc = tl.t_mix
    consts1 = [lp["norm_mix"], lp["w_in"], lp["w_out"],
               lp["lru_conv_w"], lp["lru_conv_b"], lp["lru_wa"], lp["lru_ba"], lp["lru_wx"],
               lp["lru_bx"], lp["lru_lambda"]]
    states1 = [st["lru_buf"], st["lru_h"]]
    consts2 = [lp["rwkv_mu"], lp["rwkv_w0"], lp["rwkv_w2"], lp["rwkv_a0"], lp["rwkv_a2"],
               lp["rwkv_g2"], lp["rwkv_k_k"], lp["rwkv_k_a"], lp["rwkv_r_k"], lp["rwkv_ln_w"],
               lp["rwkv_ln_b"], lp["ones_bd"]]
    states2 = [st["rwkv_shift"], st["rwkv_s"]]
    consts3 = [lp["s5_a_re"], lp["s5_a_im"], lp["s5_log_dt"], lp["s5_b_re"], lp["s5_b_im"],
               lp["s5_c_re"], lp["s5_c_im"], lp["s5_d"], lp["s5_glu_w"], lp["s5_glu_b"]]
    states3 = [st["s5_re"], st["s5_im"]]
    x_spec = pl.BlockSpec((None, tc, D_MODEL), lambda b, i: (b, i, 0))
    in_specs = ([x_spec, _batch_spec(mod)]
                + [_const_spec(a) for a in consts1] + [_batch_spec(a) for a in states1]
                + [_const_spec(a) for a in consts2] + [_batch_spec(a) for a in states2]
                + [_const_spec(a) for a in consts3] + [_batch_spec(a) for a in states3])
    state_list = states1 + states2 + states3
    out_shape = [jax.ShapeDtypeStruct(x.shape, F32)] + [jax.ShapeDtypeStruct(a.shape, F32) for a in state_list]
    out_specs = [x_spec] + [_batch_spec(a) for a in state_list]
    scratch = [pltpu.VMEM((tc + SUBLANES, LRU_WIDTH), F32),
               pltpu.VMEM((16, S5_STATES), F32), pltpu.VMEM((16, S5_STATES), F32),
               pltpu.VMEM((S5_WIDTH, S5_STATES), BF16), pltpu.VMEM((S5_WIDTH, S5_STATES), BF16)]
    scratch += [pltpu.VMEM((tc, RWKV_WIDTH), F32) for _ in range(7)]
    outs = pl.pallas_call(
        functools.partial(_mixer_kernel, tl),
        grid=(bsz, seq // tc),
        in_specs=in_specs, out_specs=out_specs, out_shape=out_shape,
        scratch_shapes=scratch,
        compiler_params=pltpu.CompilerParams(
            dimension_semantics=("arbitrary", "arbitrary"), vmem_limit_bytes=VMEM_LIMIT_BYTES),
        name="mixer",
    )(x, mod, *consts1, *states1, *consts2, *states2, *consts3, *states3)
    x_new = outs[0]
    keys = ["lru_buf", "lru_h", "rwkv_shift", "rwkv_s", "s5_re", "s5_im"]
    return x_new, dict(zip(keys, outs[1:]))


def _ffn_kernel(tl, final,
                x_ref, mod_ref, norm_ref, up_ref, cw_ref, cb_ref, down_ref, tail0_ref, fin_ref,
                xo_ref, tail_o):
    tf = tl.t_ffn
    i = pl.program_id(1)

    @pl.when(i == 0)
    def _init():
        tail_o[...] = tail0_ref[...]

    x = x_ref[...]
    row = lax.broadcasted_iota(jnp.int32, (tf, 1), 0)
    h = _rmsnorm_mod(x, norm_ref[...], mod_ref[4:5, :], mod_ref[3:4, :]).astype(BF16)

    def conv_block(c0):
        cs = slice(c0, c0 + FFN_COL_BLOCK)
        up = jnp.dot(h, up_ref[:, cs], preferred_element_type=F32)
        t2 = tail_o[SUBLANES - 2:SUBLANES - 1, cs]
        t1 = tail_o[SUBLANES - 1:SUBLANES, cs]
        m1 = jnp.where(row == 0, t1, pltpu.roll(up, 1, axis=0))
        m2 = jnp.where(row == 0, t2, jnp.where(row == 1, t1, pltpu.roll(up, 2, axis=0)))
        tail_o[:, cs] = up[tf - SUBLANES:tf, :]
        return cb_ref[:, cs] + m2 * cw_ref[0:1, cs] + m1 * cw_ref[1:2, cs] + up * cw_ref[2:3, cs]

    acc = jnp.zeros((tf, D_MODEL), F32)
    for c0 in range(0, D_FF, FFN_COL_BLOCK):
        val = conv_block(c0)
        gate = conv_block(D_FF + c0)
        act = (val * (gate * jax.nn.sigmoid(gate))).astype(BF16)
        acc = acc + jnp.dot(act, down_ref[c0:c0 + FFN_COL_BLOCK, :], preferred_element_type=F32)
    y = x + mod_ref[5:6, :] * acc
    if final:
        y = y * lax.rsqrt(jnp.mean(y * y, axis=-1, keepdims=True) + NORM_EPS) * fin_ref[...]
    xo_ref[...] = y


def _ffn(tl, final, x, mod, lp, tail0, norm_final):
    bsz, seq, _ = x.shape
    tf = tl.t_ffn
    consts = [lp["norm_ffn"], lp["ffn_up"], lp["ffn_conv_w"], lp["ffn_conv_b"], lp["ffn_down"]]
    x_spec = pl.BlockSpec((None, tf, D_MODEL), lambda b, i: (b, i, 0))
    in_specs = ([x_spec, _batch_spec(mod)] + [_const_spec(a) for a in consts]
                + [_batch_spec(tail0), _const_spec(norm_final)])
    x_new, tail = pl.pallas_call(
        functools.partial(_ffn_kernel, tl, final),
        grid=(bsz, seq // tf),
        in_specs=in_specs,
        out_specs=[x_spec, _batch_spec(tail0)],
        out_shape=[jax.ShapeDtypeStruct(x.shape, F32), jax.ShapeDtypeStruct(tail0.shape, F32)],
        compiler_params=pltpu.CompilerParams(
            dimension_semantics=("arbitrary", "arbitrary"), vmem_limit_bytes=VMEM_LIMIT_BYTES),
        name="conv_ffn",
    )(x, mod, *consts, tail0, norm_final)
    return x_new, tail


def _block_diag_dense(blocks):
    n, r, c = blocks.shape
    eye = jnp.eye(n, dtype=blocks.dtype)
    return (eye[:, None, :, None] * blocks[:, :, None, :]).reshape(n * r, n * c)


def _layer_params(p, l):
    row = lambda a: a.reshape(1, -1)
    zeros_lora = jnp.zeros((64, RWKV_WIDTH), F32)
    head = jnp.arange(RWKV_WIDTH) // RWKV_HEAD
    return {
        "norm_mix": row(p["norm_mix"][l]), "norm_ffn": row(p["norm_ffn"][l]),
        "w_in": p["w_in"][l].astype(BF16), "w_out": p["w_out"][l].astype(BF16),
        "lru_conv_w": p["lru_conv_w"][l], "lru_conv_b": row(p["lru_conv_b"][l]),
        "lru_wa": _block_diag_dense(p["lru_wa"][l]).astype(BF16), "lru_ba": row(p["lru_ba"][l]),
        "lru_wx": _block_diag_dense(p["lru_wx"][l]).astype(BF16), "lru_bx": row(p["lru_bx"][l]),
        "lru_lambda": row(p["lru_lambda"][l]),
        "rwkv_mu": row(p["rwkv_mu"][l]), "rwkv_w0": row(p["rwkv_w0"][l]),
        "rwkv_w2": jnp.concatenate([p["rwkv_w2"][l], zeros_lora], axis=0).astype(BF16),
        "rwkv_a0": row(p["rwkv_a0"][l]),
        "rwkv_a2": jnp.concatenate([zeros_lora, p["rwkv_a2"][l]], axis=0).astype(BF16),
        "rwkv_g2": p["rwkv_g2"][l].astype(BF16),
        "rwkv_k_k": row(p["rwkv_k_k"][l]), "rwkv_k_a": row(p["rwkv_k_a"][l]),
        "rwkv_r_k": row(p["rwkv_r_k"][l]), "rwkv_ln_w": row(p["rwkv_ln_w"][l]),
        "rwkv_ln_b": row(p["rwkv_ln_b"][l]),
        "ones_bd": (head[:, None] == head[None, :]).astype(F32),
        "s5_a_re": row(p["s5_a_re"][l]), "s5_a_im": row(p["s5_a_im"][l]),
        "s5_log_dt": row(jnp.repeat(p["s5_log_dt"][l], 64)),
        "s5_b_re": _block_diag_dense(jnp.swapaxes(p["s5_b_re"][l], 1, 2)),
        "s5_b_im": _block_diag_dense(jnp.swapaxes(p["s5_b_im"][l], 1, 2)),
        "s5_c_re": _block_diag_dense(jnp.swapaxes(p["s5_c_re"][l], 1, 2)).astype(BF16),
        "s5_c_im": _block_diag_dense(jnp.swapaxes(p["s5_c_im"][l], 1, 2)).astype(BF16),
        "s5_d": row(p["s5_d"][l]),
        "s5_glu_w": p["s5_glu_w"][l].astype(BF16), "s5_glu_b": row(p["s5_glu_b"][l]),
        "ffn_up": p["ffn_up"][l].astype(BF16), "ffn_conv_w": p["ffn_conv_w"][l],
        "ffn_conv_b": row(p["ffn_conv_b"][l]), "ffn_down": p["ffn_down"][l].astype(BF16),
    }


def _pad_tail(buf):
    return jnp.pad(buf, ((0, 0), (SUBLANES - buf.shape[1], 0), (0, 0)))


def _state_in(states, l):
    lru_buf, lru_h, rw_shift, rw_s, s5_re, s5_im, ffn_buf = (s[l] for s in states)
    bsz = lru_h.shape[0]
    return {
        "lru_buf": _pad_tail(lru_buf),
        "lru_h": lru_h.reshape(bsz, 1, LRU_WIDTH),
        "rwkv_shift": rw_shift.reshape(bsz, 1, RWKV_COLS),
        "rwkv_s": jnp.transpose(rw_s, (0, 2, 1, 3)).reshape(bsz, RWKV_HEAD, RWKV_WIDTH),
        "s5_re": s5_re.reshape(bsz, 1, S5_STATES),
        "s5_im": s5_im.reshape(bsz, 1, S5_STATES),
    }, _pad_tail(ffn_buf)


def _state_out(st, ffn_tail):
    bsz = st["lru_h"].shape[0]
    return (st["lru_buf"][:, SUBLANES - (LRU_CONV - 1):, :],
            st["lru_h"].reshape(bsz, LRU_WIDTH),
            st["rwkv_shift"].reshape(bsz, RWKV_COLS),
            jnp.transpose(st["rwkv_s"].reshape(bsz, RWKV_HEAD, RWKV_HEADS, RWKV_HEAD), (0, 2, 1, 3)),
            st["s5_re"].reshape(bsz, 16, 64),
            st["s5_im"].reshape(bsz, 16, 64),
            ffn_tail[:, SUBLANES - 2:, :])


def _trunk(x, mods, states, layer_params, norm_final):
    depth = len(layer_params)
    tl = _tiling(x.shape[1])
    new = []
    for l in range(depth):
        st, ffn_tail0 = _state_in(states, l)
        x, st = _mixer(tl, x, mods[l], layer_params[l], st)
        x, ffn_tail = _ffn(tl, l == depth - 1, x, mods[l], layer_params[l], ffn_tail0, norm_final)
        new.append(_state_out(st, ffn_tail))
    stacked = tuple(jnp.stack([n[j] for n in new], axis=0) for j in range(7))
    return x, stacked


def _zero_states(depth, bsz):
    return (jnp.zeros((depth, bsz, LRU_CONV - 1, LRU_WIDTH), F32),
            jnp.zeros((depth, bsz, LRU_WIDTH), F32),
            jnp.zeros((depth, bsz, RWKV_COLS), F32),
            jnp.zeros((depth, bsz, RWKV_HEADS, RWKV_HEAD, RWKV_HEAD), F32),
            jnp.zeros((depth, bsz, 16, 64), F32),
            jnp.zeros((depth, bsz, 16, 64), F32),
            jnp.zeros((depth, bsz, 2, 2 * D_FF), F32))


def kernel(x_prompt, x_sample, c_prompt, c_sample, state_lru_conv, state_lru_h, state_rwkv_shift, state_rwkv_S, state_s5_re, state_s5_im, state_ffn_conv, w_ada, b_ada, norm_mix, norm_ffn, w_in, w_out, lru_conv_w, lru_conv_b, lru_wa, lru_ba, lru_wx, lru_bx, lru_lambda, rwkv_mu, rwkv_w0, rwkv_w2, rwkv_a0, rwkv_a2, rwkv_g2, rwkv_k_k, rwkv_k_a, rwkv_r_k, rwkv_ln_w, rwkv_ln_b, s5_a_re, s5_a_im, s5_b_re, s5_b_im, s5_c_re, s5_c_im, s5_d, s5_log_dt, s5_glu_w, s5_glu_b, ffn_up, ffn_conv_w, ffn_conv_b, ffn_down, norm_final):
    p = dict(norm_mix=norm_mix, norm_ffn=norm_ffn, w_in=w_in, w_out=w_out,
             lru_conv_w=lru_conv_w, lru_conv_b=lru_conv_b, lru_wa=lru_wa, lru_ba=lru_ba,
             lru_wx=lru_wx, lru_bx=lru_bx, lru_lambda=lru_lambda,
             rwkv_mu=rwkv_mu, rwkv_w0=rwkv_w0, rwkv_w2=rwkv_w2, rwkv_a0=rwkv_a0, rwkv_a2=rwkv_a2,
             rwkv_g2=rwkv_g2, rwkv_k_k=rwkv_k_k, rwkv_k_a=rwkv_k_a,
             rwkv_r_k=rwkv_r_k.reshape(rwkv_r_k.shape[0], RWKV_WIDTH),
             rwkv_ln_w=rwkv_ln_w, rwkv_ln_b=rwkv_ln_b,
             s5_a_re=s5_a_re, s5_a_im=s5_a_im, s5_b_re=s5_b_re, s5_b_im=s5_b_im,
             s5_c_re=s5_c_re, s5_c_im=s5_c_im, s5_d=s5_d, s5_log_dt=s5_log_dt,
             s5_glu_w=s5_glu_w, s5_glu_b=s5_glu_b,
             ffn_up=ffn_up, ffn_conv_w=ffn_conv_w, ffn_conv_b=ffn_conv_b, ffn_down=ffn_down)
    depth = w_in.shape[0]
    layer_params = [_layer_params(p, l) for l in range(depth)]
    fin = norm_final.reshape(1, D_MODEL)
    n_prompt = c_prompt.shape[0]
    mod = _ada(jnp.concatenate([c_prompt, c_sample], axis=0), w_ada, b_ada)
    mod = mod.reshape(depth, mod.shape[1], 6, D_MODEL)
    mods_p = [mod[l, :n_prompt] for l in range(depth)]
    mods_s = [mod[l, n_prompt:] for l in range(depth)]
    y_prompt, p_states = _trunk(x_prompt, mods_p, _zero_states(depth, x_prompt.shape[0]), layer_params, fin)
    s_in = (state_lru_conv, state_lru_h, state_rwkv_shift, state_rwkv_S, state_s5_re, state_s5_im,
            state_ffn_conv)
    y_sample, s_states = _trunk(x_sample, mods_s, s_in, layer_params, fin)
    return (y_prompt, y_sample) + tuple(p_states) + tuple(s_states)
```

```python
import functools
import math
from typing import NamedTuple

import jax
import jax.numpy as jnp
from jax import lax
from jax.experimental import pallas as pl
from jax.experimental.pallas import tpu as pltpu

F32 = jnp.float32
BF16 = jnp.bfloat16

D_MODEL = 1024
LRU_WIDTH = 256
LRU_CONV = 4
LRU_C = 8.0
RWKV_WIDTH = 512
RWKV_HEAD = 64
RWKV_HEADS = 8
RWKV_COLS = 1792
S5_WIDTH = 256
S5_STATES = 1024
IN_COLS = 2560
D_FF = 2816
NORM_EPS = 1e-6
RWKV_GN_EPS = 64e-5
SUBLANES = 8
FFN_COL_BLOCK = 256
VMEM_LIMIT_BYTES = 56 * 1024 * 1024


class Tiling(NamedTuple):
    t_mix: int
    chunk: int
    group: int
    t_ffn: int


def _tiling(seq_len):
    t_mix = min(seq_len, 256)
    chunk = min(t_mix, 64)
    group = min(RWKV_HEADS, 256 // chunk)
    return Tiling(t_mix, chunk, group, min(seq_len, 512))


def _dg(a, b, ca, cb, precision=None):
    return lax.dot_general(a, b, (((ca,), (cb,)), ((), ())), precision=precision,
                           preferred_element_type=F32)


def _mm(a, w):
    return jnp.dot(a.astype(BF16), w, preferred_element_type=F32)


def _expm1(x):
    u = jnp.exp(x)
    small = u >= 0.5
    stable = (u - 1.0) * x / jnp.log(jnp.where(small, u, 0.75))
    return jnp.where(u == 1.0, x, jnp.where(small, stable, u - 1.0))


def _shift_rows(x, d, fill):
    if d % SUBLANES == 0:
        return jnp.concatenate([jnp.full((d, x.shape[1]), fill, x.dtype), x[:x.shape[0] - d]], axis=0)
    rolled = pltpu.roll(x, d, axis=0)
    row = lax.broadcasted_iota(jnp.int32, x.shape, 0)
    return jnp.where(row >= d, rolled, fill)


def _shift_in(x, d, tail):
    rolled = pltpu.roll(x, d, axis=0)
    head = rolled[0:SUBLANES]
    r8 = lax.broadcasted_iota(jnp.int32, (SUBLANES, 1), 0)
    for j in range(d):
        head = jnp.where(r8 == j, tail[SUBLANES - d + j:SUBLANES - d + j + 1], head)
    return jnp.concatenate([head, rolled[SUBLANES:]], axis=0)


def _linear_scan(a, b):
    d = 1
    while d < a.shape[0]:
        b = a * _shift_rows(b, d, 0.0) + b
        a = a * _shift_rows(a, d, 1.0)
        d *= 2
    return b


def _block_mask(rows, cols, rb, cb, dtype):
    r = lax.broadcasted_iota(jnp.int32, (rows, cols), 0) >> int(math.log2(rb))
    c = lax.broadcasted_iota(jnp.int32, (rows, cols), 1) >> int(math.log2(cb))
    return jnp.where(r == c, 1.0, 0.0).astype(dtype)


def _split(x):
    hi = x.astype(BF16)
    return hi, (x - hi.astype(F32)).astype(BF16)


def _lhs3(a, axis):
    hi, lo = _split(a)
    return jnp.concatenate([hi, hi, lo], axis=axis)


def _rhs3(b, axis, group=1):
    hi, lo = _split(b)
    if group > 1:
        rows, cols = b.shape
        mask = _block_mask(group * rows, cols, rows, cols // group, BF16)
        hi = jnp.concatenate([hi] * group, axis=0) * mask
        lo = jnp.concatenate([lo] * group, axis=0) * mask
    return jnp.concatenate([hi, lo, hi], axis=axis)


def _fold_diag(m, group):
    rows = m.shape[0] // group
    width = m.shape[1] // group
    rb = lax.broadcasted_iota(jnp.int32, m.shape, 0) >> int(math.log2(rows))
    cb = lax.broadcasted_iota(jnp.int32, m.shape, 1) >> int(math.log2(width))
    m = jnp.where(rb == cb, m, 0.0)
    out = m[0:rows]
    for h in range(1, group):
        out = out + m[h * rows:(h + 1) * rows]
    return out


def _rmsnorm_mod(x, gain, scale, shift):
    y = x * lax.rsqrt(jnp.mean(x * x, axis=-1, keepdims=True) + NORM_EPS)
    return (y * gain) * (1.0 + scale) + shift


def _ada_kernel(c_ref, w_ref, b_ref, o_ref):
    c = c_ref[...]
    o_ref[...] = _mm(c * jax.nn.sigmoid(c), w_ref[...].astype(BF16)) + b_ref[...]


def _ada(c, w_ada, b_ada):
    depth, _, n = w_ada.shape
    rows = c.shape[0]
    tn = 1536
    return pl.pallas_call(
        _ada_kernel,
        grid=(depth, n // tn),
        in_specs=[pl.BlockSpec((rows, D_MODEL), lambda l, j: (0, 0)),
                  pl.BlockSpec((None, D_MODEL, tn), lambda l, j: (l, 0, j)),
                  pl.BlockSpec((None, 1, tn), lambda l, j: (l, 0, j))],
        out_specs=pl.BlockSpec((None, rows, tn), lambda l, j: (l, 0, j)),
        out_shape=jax.ShapeDtypeStruct((depth, rows, n), F32),
        compiler_params=pltpu.CompilerParams(
            dimension_semantics=("arbitrary", "arbitrary"), vmem_limit_bytes=VMEM_LIMIT_BYTES),
        name="ada_mod",
    )(c, w_ada, b_ada.reshape(depth, 1, n))


def _mixer_kernel(tl,
                  x_ref, mod_ref, norm_ref, w_in_ref, w_out_ref,
                  conv_w_ref, conv_b_ref, wa_ref, ba_ref, wx_ref, bx_ref, lam_ref,
                  lru_buf0_ref, lru_h0_ref,
                  mu_ref, w0_ref, w2_ref, a0_ref, a2_ref, g2_ref, kk_ref, ka_ref, rk_ref,
                  lnw_ref, lnb_ref, ones_ref, shift0_ref, s0_ref,
                  are_ref, aim_ref, ldt_ref, bre_ref, bim_ref, cre_ref, cim_ref, d_ref,
                  gluw_ref, glub_ref, s5re0_ref, s5im0_ref,
                  xo_ref, lru_buf_o, lru_h_o, shift_o, s_o, s5re_o, s5im_o,
                  pw_re, pw_im, bbar_re, bbar_im):
    tc, chunk, group = tl.t_mix, tl.chunk, tl.group
    n_scan = int(math.log2(tc))
    i = pl.program_id(1)

    @pl.when(i == 0)
    def _init():
        lru_buf_o[...] = lru_buf0_ref[...]
        lru_h_o[...] = lru_h0_ref[...]
        shift_o[...] = shift0_ref[...]
        s_o[...] = s0_ref[...]
        s5re_o[...] = s5re0_ref[...]
        s5im_o[...] = s5im0_ref[...]
        dt = jnp.exp(ldt_ref[...])
        a_re = are_ref[...]
        a_im = aim_ref[...]
        mag = jnp.exp(dt * a_re)
        abr = mag * jnp.cos(dt * a_im)
        abi = mag * jnp.sin(dt * a_im)
        den = a_re * a_re + a_im * a_im
        fr = ((abr - 1.0) * a_re + abi * a_im) / den
        fi = (abi * a_re - (abr - 1.0) * a_im) / den
        bbar_re[...] = (fr * bre_ref[...] - fi * bim_ref[...]).astype(BF16)
        bbar_im[...] = (fr * bim_ref[...] + fi * bre_ref[...]).astype(BF16)
        pr, pi = abr, abi
        for k in range(max(n_scan, 1)):
            pw_re[k:k + 1, :] = pr
            pw_im[k:k + 1, :] = pi
            pr, pi = pr * pr - pi * pi, 2.0 * pr * pi

    x = x_ref[...]
    row = lax.broadcasted_iota(jnp.int32, (tc, 1), 0)
    h = _rmsnorm_mod(x, norm_ref[...], mod_ref[1:2, :], mod_ref[0:1, :])
    proj = _mm(h, w_in_ref[...])
    u_gate = proj[:, 0:256]
    u_lru = proj[:, 256:512]
    p_rw = proj[:, 512:2304]
    u_s5 = proj[:, 2304:2560]

    cw = conv_w_ref[...]
    lru_tail = lru_buf_o[...]
    xc = conv_b_ref[...] + _shift_in(u_lru, LRU_CONV - 1, lru_tail) * cw[0:1]
    for k in range(1, LRU_CONV - 1):
        xc = xc + _shift_in(u_lru, LRU_CONV - 1 - k, lru_tail) * cw[k:k + 1]
    xc = xc + u_lru * cw[LRU_CONV - 1:LRU_CONV]
    lru_buf_o[...] = u_lru[tc - SUBLANES:tc, :]
    r_gate = jax.nn.sigmoid(_mm(xc, wa_ref[...]) + ba_ref[...])
    i_gate = jax.nn.sigmoid(_mm(xc, wx_ref[...]) + bx_ref[...])
    log_a = -LRU_C * r_gate * jax.nn.softplus(-lam_ref[...])
    a = jnp.exp(log_a)
    gain = jnp.sqrt(jnp.maximum(-_expm1(2.0 * log_a), 0.0))
    b = gain * i_gate * xc
    b = b + jnp.where(row == 0, a * lru_h_o[...], 0.0)
    h_lru = _linear_scan(a, b)
    lru_h_o[...] = h_lru[tc - 1:tc, :]
    y_a = h_lru * jax.nn.gelu(u_gate)

    rolled = pltpu.roll(p_rw, 1, axis=0)
    r8 = lax.broadcasted_iota(jnp.int32, (SUBLANES, 1), 0)
    prev = jnp.concatenate([jnp.where(r8 == 0, shift_o[...], rolled[0:SUBLANES]), rolled[SUBLANES:]], axis=0)
    shift_o[...] = p_rw[tc - 1:tc, :]
    xm = p_rw + (prev - p_rw) * mu_ref[...]
    r_t = xm[:, 0:512]
    k_t = xm[:, 512:1024]
    v_t = xm[:, 1024:1536]
    lo = xm[:, 1536:1664]
    g_lo = xm[:, 1664:1792]
    ones2 = ones_ref[...]

    def head_sum(z):
        hi, lo = _split(z)
        return _dg(jnp.concatenate([hi, lo], axis=1), ones2, 1, 0)

    log_w = -jax.nn.softplus(-(w0_ref[...] + _mm(jnp.tanh(lo), w2_ref[...]))) - 0.5
    a_t = jax.nn.sigmoid(a0_ref[...] + _mm(lo, a2_ref[...]))
    g_t = _mm(jax.nn.sigmoid(g_lo), g2_ref[...])
    kk = k_t * kk_ref[...]
    kk = kk * lax.rsqrt(jnp.maximum(head_sum(kk * kk), 1e-24))
    kp = k_t * (1.0 + (a_t - 1.0) * ka_ref[...])
    bq_t = kk * a_t
    ld_t = -jnp.exp(log_w)

    gw = group * RWKV_HEAD
    crow = lax.broadcasted_iota(jnp.int32, (chunk, group * chunk), 0)
    ccol = lax.broadcasted_iota(jnp.int32, (chunk, group * chunk), 1) & (chunk - 1)
    strict = ccol < crow
    incl = ccol <= crow
    diag = ccol == crow
    n_pow = int(math.log2(chunk))

    n_chunks = tc // chunk
    n_groups = RWKV_HEADS // group
    lanes = [slice(g * gw, (g + 1) * gw) for g in range(n_groups)]
    chains = [(c, g) for c in range(n_chunks) for g in range(n_groups)]

    a3, bb3, kb3, v3, kgbg3, v_c, g_end_c = {}, {}, {}, {}, {}, {}, {}
    for c in range(n_chunks):
        sl = slice(c * chunk, (c + 1) * chunk)
        r, kpc, v, kkc, bq, ld = r_t[sl], kp[sl], v_t[sl], kk[sl], bq_t[sl], ld_t[sl]
        cum = ld
        d = 1
        while d < chunk:
            cum = cum + _shift_rows(cum, d, 0.0)
            d *= 2
        cum_end = cum[chunk - 1:chunk, :]
        inv = jnp.exp(-cum)
        tail_decay = jnp.exp(cum_end - cum)
        rt = r * jnp.exp(cum)
        kkt = kkc * jnp.exp(cum - ld)
        kb = kpc * inv
        bb = bq * inv
        kg = kpc * tail_decay
        bg = bq * tail_decay
        g_end_c[c] = jnp.exp(cum_end)
        for g, ls in enumerate(lanes):
            a3[c, g] = _lhs3(jnp.concatenate([kkt[:, ls], rt[:, ls]], axis=0), 1)
            bb3[c, g] = _rhs3(bb[:, ls], 1, group)
            kb3[c, g] = _rhs3(kb[:, ls], 1, group)
            v3[c, g] = _rhs3(v[:, ls], 0, group)
            kgbg3[c, g] = _rhs3(jnp.concatenate([kg[:, ls], -bg[:, ls]], axis=0), 0)
            v_c[c, g] = v[:, ls]

    xb = {ch: _dg(a3[ch], bb3[ch], 1, 1) for ch in chains}
    xk = {ch: _dg(a3[ch], kb3[ch], 1, 1) for ch in chains}
    n = {ch: jnp.where(strict, -xb[ch][0:chunk], 0.0) for ch in chains}
    rb3 = {ch: _lhs3(jnp.where(incl, xb[ch][chunk:], 0.0), 1) for ch in chains}
    lv = {ch: _dg(_lhs3(jnp.concatenate([jnp.where(strict, xk[ch][0:chunk], 0.0),
                                          jnp.where(incl, xk[ch][chunk:], 0.0)], axis=0), 1),
                  v3[ch], 1, 0) for ch in chains}
    t = {ch: jnp.where(diag, 1.0, n[ch]) for ch in chains}
    p = {ch: _dg(_lhs3(n[ch], 1), _rhs3(n[ch], 0, group), 1, 0) for ch in chains}
    for q in range(1, n_pow):
        last = q + 1 == n_pow
        tp = {ch: _dg(_lhs3(t[ch] if last else jnp.concatenate([t[ch], p[ch]], axis=0), 1),
                      _rhs3(p[ch], 0, group), 1, 0) for ch in chains}
        t = {ch: t[ch] + tp[ch][0:chunk] for ch in chains}
        if not last:
            p = {ch: tp[ch][chunk:] for ch in chains}
    t3 = {ch: _lhs3(t[ch], 1) for ch in chains}

    y_rows = []
    for c in range(n_chunks):
        s_prev = [s_o[:, ls] for ls in lanes]
        xs = [_dg(a3[c, g], _rhs3(s_prev[g], 1, group), 1, 1) for g in range(n_groups)]
        u = [_dg(t3[c, g], _rhs3(xs[g][0:chunk] + lv[c, g][0:chunk], 0, group), 1, 0)
             for g in range(n_groups)]
        ru = [_dg(rb3[c, g], _rhs3(u[g], 0, group), 1, 0) for g in range(n_groups)]
        upd = [_dg(_lhs3(jnp.concatenate([v_c[c, g], u[g]], axis=0), 0), kgbg3[c, g], 0, 0)
               for g in range(n_groups)]
        y_rows.append(jnp.concatenate(
            [xs[g][chunk:] + lv[c, g][chunk:] - ru[g] for g in range(n_groups)], axis=1))
        for g, ls in enumerate(lanes):
            s_o[:, ls] = s_prev[g] * g_end_c[c][:, ls] + _fold_diag(upd[g], group)
    y = jnp.concatenate(y_rows, axis=0)
    inv_n = 1.0 / RWKV_HEAD
    mean = head_sum(y) * inv_n
    yc = y - mean
    var = head_sum(yc * yc) * inv_n
    yn = yc * lax.rsqrt(var + RWKV_GN_EPS) * lnw_ref[...] + lnb_ref[...]
    bonus = head_sum(r_t * kp * rk_ref[...]) * v_t
    y_b = (yn + bonus) * g_t

    bu_re = _mm(u_s5, bbar_re[...])
    bu_im = _mm(u_s5, bbar_im[...])
    h0r = s5re_o[...]
    h0i = s5im_o[...]
    ar = pw_re[0:1, :]
    ai = pw_im[0:1, :]
    first = row == 0
    hr = bu_re + jnp.where(first, ar * h0r - ai * h0i, 0.0)
    hi = bu_im + jnp.where(first, ar * h0i + ai * h0r, 0.0)
    d = 1
    for k in range(n_scan):
        pr = pw_re[k:k + 1, :]
        pi = pw_im[k:k + 1, :]
        sr = _shift_rows(hr, d, 0.0)
        si = _shift_rows(hi, d, 0.0)
        hr, hi = hr + (pr * sr - pi * si), hi + (pr * si + pi * sr)
        d *= 2
    s5re_o[...] = hr[tc - 1:tc, :]
    s5im_o[...] = hi[tc - 1:tc, :]
    y5 = _mm(hr, cre_ref[...]) - _mm(hi, cim_ref[...]) + d_ref[...] * u_s5
    z = jax.nn.gelu(y5)
    y_c = z * jax.nn.sigmoid(_mm(z, gluw_ref[...]) + glub_ref[...])

    mix = _mm(jnp.concatenate([y_a, y_b, y_c], axis=-1), w_out_ref[...])
    xo_ref[...] = x + mod_ref[2:3, :] * mix


def _const_spec(arr):
    nd = arr.ndim
    return pl.BlockSpec(arr.shape, lambda b, i: (0,) * nd)


def _batch_spec(arr):
    nd = arr.ndim
    return pl.BlockSpec((None,) + arr.shape[1:], lambda b, i: (b,) + (0,) * (nd - 1))


def _mixer(tl, x, mod, lp, st):
    bsz, seq, _ = x.shape
    tc = tl.t_mix
    consts1 = [lp["norm_mix"], lp["w_in"], lp["w_out"],
               lp["lru_conv_w"], lp["lru_conv_b"], lp["lru_wa"], lp["lru_ba"], lp["lru_wx"],
               lp["lru_bx"], lp["lru_lambda"]]
    states1 = [st["lru_buf"], st["lru_h"]]
    consts2 = [lp["rwkv_mu"], lp["rwkv_w0"], lp["rwkv_w2"], lp["rwkv_a0"], lp["rwkv_a2"],
               lp["rwkv_g2"], lp["rwkv_k_k"], lp["rwkv_k_a"], lp["rwkv_r_k"], lp["rwkv_ln_w"],
               lp["rwkv_ln_b"], lp["ones_bd"]]
    states2 = [st["rwkv_shift"], st["rwkv_s"]]
    consts3 = [lp["s5_a_re"], lp["s5_a_im"], lp["s5_log_dt"], lp["s5_b_re"], lp["s5_b_im"],
               lp["s5_c_re"], lp["s5_c_im"], lp["s5_d"], lp["s5_glu_w"], lp["s5_glu_b"]]
    states3 = [st["s5_re"], st["s5_im"]]
    x_spec = pl.BlockSpec((None, tc, D_MODEL), lambda b, i: (b, i, 0))
    in_specs = ([x_spec, _batch_spec(mod)]
                + [_const_spec(a) for a in consts1] + [_batch_spec(a) for a in states1]
                + [_const_spec(a) for a in consts2] + [_batch_spec(a) for a in states2]
                + [_const_spec(a) for a in consts3] + [_batch_spec(a) for a in states3])
    state_list = states1 + states2 + states3
    out_shape = [jax.ShapeDtypeStruct(x.shape, F32)] + [jax.ShapeDtypeStruct(a.shape, F32) for a in state_list]
    out_specs = [x_spec] + [_batch_spec(a) for a in state_list]
    scratch = [pltpu.VMEM((16, S5_STATES), F32), pltpu.VMEM((16, S5_STATES), F32),
               pltpu.VMEM((S5_WIDTH, S5_STATES), BF16), pltpu.VMEM((S5_WIDTH, S5_STATES), BF16)]
    outs = pl.pallas_call(
        functools.partial(_mixer_kernel, tl),
        grid=(bsz, seq // tc),
        in_specs=in_specs, out_specs=out_specs, out_shape=out_shape,
        scratch_shapes=scratch,
        compiler_params=pltpu.CompilerParams(
            dimension_semantics=("arbitrary", "arbitrary"), vmem_limit_bytes=VMEM_LIMIT_BYTES),
        name="mixer",
    )(x, mod, *consts1, *states1, *consts2, *states2, *consts3, *states3)
    x_new = outs[0]
    keys = ["lru_buf", "lru_h", "rwkv_shift", "rwkv_s", "s5_re", "s5_im"]
    return x_new, dict(zip(keys, outs[1:]))


def _ffn_kernel(tl, final,
                x_ref, mod_ref, norm_ref, up_ref, cw_ref, cb_ref, down_ref, tail0_ref, fin_ref,
                xo_ref, tail_o):
    tf = tl.t_ffn
    i = pl.program_id(1)

    @pl.when(i == 0)
    def _init():
        tail_o[...] = tail0_ref[...]

    x = x_ref[...]
    h = _rmsnorm_mod(x, norm_ref[...], mod_ref[4:5, :], mod_ref[3:4, :]).astype(BF16)

    def conv_block(c0):
        cs = slice(c0, c0 + FFN_COL_BLOCK)
        up = jnp.dot(h, up_ref[:, cs], preferred_element_type=F32)
        tail = tail_o[:, cs]
        m1 = _shift_in(up, 1, tail)
        m2 = _shift_in(up, 2, tail)
        tail_o[:, cs] = up[tf - SUBLANES:tf, :]
        return cb_ref[:, cs] + m2 * cw_ref[0:1, cs] + m1 * cw_ref[1:2, cs] + up * cw_ref[2:3, cs]

    acc = jnp.zeros((tf, D_MODEL), F32)
    for c0 in range(0, D_FF, FFN_COL_BLOCK):
        val = conv_block(c0)
        gate = conv_block(D_FF + c0)
        act = (val * (gate * jax.nn.sigmoid(gate))).astype(BF16)
        acc = acc + jnp.dot(act, down_ref[c0:c0 + FFN_COL_BLOCK, :], preferred_element_type=F32)
    y = x + mod_ref[5:6, :] * acc
    if final:
        y = y * lax.rsqrt(jnp.mean(y * y, axis=-1, keepdims=True) + NORM_EPS) * fin_ref[...]
    xo_ref[...] = y


def _ffn(tl, final, x, mod, lp, tail0, norm_final):
    bsz, seq, _ = x.shape
    tf = tl.t_ffn
    consts = [lp["norm_ffn"], lp["ffn_up"], lp["ffn_conv_w"], lp["ffn_conv_b"], lp["ffn_down"]]
    x_spec = pl.BlockSpec((None, tf, D_MODEL), lambda b, i: (b, i, 0))
    in_specs = ([x_spec, _batch_spec(mod)] + [_const_spec(a) for a in consts]
                + [_batch_spec(tail0), _const_spec(norm_final)])
    x_new, tail = pl.pallas_call(
        functools.partial(_ffn_kernel, tl, final),
        grid=(bsz, seq // tf),
        in_specs=in_specs,
        out_specs=[x_spec, _batch_spec(tail0)],
        out_shape=[jax.ShapeDtypeStruct(x.shape, F32), jax.ShapeDtypeStruct(tail0.shape, F32)],
        compiler_params=pltpu.CompilerParams(
            dimension_semantics=("arbitrary", "arbitrary"), vmem_limit_bytes=VMEM_LIMIT_BYTES),
        name="conv_ffn",
    )(x, mod, *consts, tail0, norm_final)
    return x_new, tail


def _block_diag_dense(blocks):
    n, r, c = blocks.shape
    eye = jnp.eye(n, dtype=blocks.dtype)
    return (eye[:, None, :, None] * blocks[:, :, None, :]).reshape(n * r, n * c)


def _layer_params(p, l):
    row = lambda a: a.reshape(1, -1)
    zeros_lora = jnp.zeros((64, RWKV_WIDTH), F32)
    head = jnp.arange(RWKV_WIDTH) // RWKV_HEAD
    return {
        "norm_mix": row(p["norm_mix"][l]), "norm_ffn": row(p["norm_ffn"][l]),
        "w_in": p["w_in"][l].astype(BF16), "w_out": p["w_out"][l].astype(BF16),
        "lru_conv_w": p["lru_conv_w"][l], "lru_conv_b": row(p["lru_conv_b"][l]),
        "lru_wa": _block_diag_dense(p["lru_wa"][l]).astype(BF16), "lru_ba": row(p["lru_ba"][l]),
        "lru_wx": _block_diag_dense(p["lru_wx"][l]).astype(BF16), "lru_bx": row(p["lru_bx"][l]),
        "lru_lambda": row(p["lru_lambda"][l]),
        "rwkv_mu": row(p["rwkv_mu"][l]), "rwkv_w0": row(p["rwkv_w0"][l]),
        "rwkv_w2": jnp.concatenate([p["rwkv_w2"][l], zeros_lora], axis=0).astype(BF16),
        "rwkv_a0": row(p["rwkv_a0"][l]),
        "rwkv_a2": jnp.concatenate([zeros_lora, p["rwkv_a2"][l]], axis=0).astype(BF16),
        "rwkv_g2": p["rwkv_g2"][l].astype(BF16),
        "rwkv_k_k": row(p["rwkv_k_k"][l]), "rwkv_k_a": row(p["rwkv_k_a"][l]),
        "rwkv_r_k": row(p["rwkv_r_k"][l]), "rwkv_ln_w": row(p["rwkv_ln_w"][l]),
        "rwkv_ln_b": row(p["rwkv_ln_b"][l]),
        "ones_bd": jnp.tile(head[:, None] == head[None, :], (2, 1)).astype(BF16),
        "s5_a_re": row(p["s5_a_re"][l]), "s5_a_im": row(p["s5_a_im"][l]),
        "s5_log_dt": row(jnp.repeat(p["s5_log_dt"][l], 64)),
        "s5_b_re": _block_diag_dense(jnp.swapaxes(p["s5_b_re"][l], 1, 2)),
        "s5_b_im": _block_diag_dense(jnp.swapaxes(p["s5_b_im"][l], 1, 2)),
        "s5_c_re": _block_diag_dense(jnp.swapaxes(p["s5_c_re"][l], 1, 2)).astype(BF16),
        "s5_c_im": _block_diag_dense(jnp.swapaxes(p["s5_c_im"][l], 1, 2)).astype(BF16),
        "s5_d": row(p["s5_d"][l]),
        "s5_glu_w": p["s5_glu_w"][l].astype(BF16), "s5_glu_b": row(p["s5_glu_b"][l]),
        "ffn_up": p["ffn_up"][l].astype(BF16), "ffn_conv_w": p["ffn_conv_w"][l],
        "ffn_conv_b": row(p["ffn_conv_b"][l]), "ffn_down": p["ffn_down"][l].astype(BF16),
    }


def _pad_tail(buf):
    return jnp.pad(buf, ((0, 0), (SUBLANES - buf.shape[1], 0), (0, 0)))


def _state_in(states, l):
    lru_buf, lru_h, rw_shift, rw_s, s5_re, s5_im, ffn_buf = (s[l] for s in states)
    bsz = lru_h.shape[0]
    return {
        "lru_buf": _pad_tail(lru_buf),
        "lru_h": lru_h.reshape(bsz, 1, LRU_WIDTH),
        "rwkv_shift": rw_shift.reshape(bsz, 1, RWKV_COLS),
        "rwkv_s": jnp.transpose(rw_s, (0, 2, 1, 3)).reshape(bsz, RWKV_HEAD, RWKV_WIDTH),
        "s5_re": s5_re.reshape(bsz, 1, S5_STATES),
        "s5_im": s5_im.reshape(bsz, 1, S5_STATES),
    }, _pad_tail(ffn_buf)


def _state_out(st, ffn_tail):
    bsz = st["lru_h"].shape[0]
    return (st["lru_buf"][:, SUBLANES - (LRU_CONV - 1):, :],
            st["lru_h"].reshape(bsz, LRU_WIDTH),
            st["rwkv_shift"].reshape(bsz, RWKV_COLS),
            jnp.transpose(st["rwkv_s"].reshape(bsz, RWKV_HEAD, RWKV_HEADS, RWKV_HEAD), (0, 2, 1, 3)),
            st["s5_re"].reshape(bsz, 16, 64),
            st["s5_im"].reshape(bsz, 16, 64),
            ffn_tail[:, SUBLANES - 2:, :])


def _trunk(x, mods, states, layer_params, norm_final):
    depth = len(layer_params)
    tl = _tiling(x.shape[1])
    new = []
    for l in range(depth):
        st, ffn_tail0 = _state_in(states, l)
        x, st = _mixer(tl, x, mods[l], layer_params[l], st)
        x, ffn_tail = _ffn(tl, l == depth - 1, x, mods[l], layer_params[l], ffn_tail0, norm_final)
        new.append(_state_out(st, ffn_tail))
    stacked = tuple(jnp.stack([n[j] for n in new], axis=0) for j in range(7))
    return x, stacked


def _zero_states(depth, bsz):
    return (jnp.zeros((depth, bsz, LRU_CONV - 1, LRU_WIDTH), F32),
            jnp.zeros((depth, bsz, LRU_WIDTH), F32),
            jnp.zeros((depth, bsz, RWKV_COLS), F32),
            jnp.zeros((depth, bsz, RWKV_HEADS, RWKV_HEAD, RWKV_HEAD), F32),
            jnp.zeros((depth, bsz, 16, 64), F32),
            jnp.zeros((depth, bsz, 16, 64), F32),
            jnp.zeros((depth, bsz, 2, 2 * D_FF), F32))


def kernel(x_prompt, x_sample, c_prompt, c_sample, state_lru_conv, state_lru_h, state_rwkv_shift, state_rwkv_S, state_s5_re, state_s5_im, state_ffn_conv, w_ada, b_ada, norm_mix, norm_ffn, w_in, w_out, lru_conv_w, lru_conv_b, lru_wa, lru_ba, lru_wx, lru_bx, lru_lambda, rwkv_mu, rwkv_w0, rwkv_w2, rwkv_a0, rwkv_a2, rwkv_g2, rwkv_k_k, rwkv_k_a, rwkv_r_k, rwkv_ln_w, rwkv_ln_b, s5_a_re, s5_a_im, s5_b_re, s5_b_im, s5_c_re, s5_c_im, s5_d, s5_log_dt, s5_glu_w, s5_glu_b, ffn_up, ffn_conv_w, ffn_conv_b, ffn_down, norm_final):
    p = dict(norm_mix=norm_mix, norm_ffn=norm_ffn, w_in=w_in, w_out=w_out,
             lru_conv_w=lru_conv_w, lru_conv_b=lru_conv_b, lru_wa=lru_wa, lru_ba=lru_ba,
             lru_wx=lru_wx, lru_bx=lru_bx, lru_lambda=lru_lambda,
             rwkv_mu=rwkv_mu, rwkv_w0=rwkv_w0, rwkv_w2=rwkv_w2, rwkv_a0=rwkv_a0, rwkv_a2=rwkv_a2,
             rwkv_g2=rwkv_g2, rwkv_k_k=rwkv_k_k, rwkv_k_a=rwkv_k_a,
             rwkv_r_k=rwkv_r_k.reshape(rwkv_r_k.shape[0], RWKV_WIDTH),
             rwkv_ln_w=rwkv_ln_w, rwkv_ln_b=rwkv_ln_b,
             s5_a_re=s5_a_re, s5_a_im=s5_a_im, s5_b_re=s5_b_re, s5_b_im=s5_b_im,
             s5_c_re=s5_c_re, s5_c_im=s5_c_im, s5_d=s5_d, s5_log_dt=s5_log_dt,
             s5_glu_w=s5_glu_w, s5_glu_b=s5_glu_b,
             ffn_up=ffn_up, ffn_conv_w=ffn_conv_w, ffn_conv_b=ffn_conv_b, ffn_down=ffn_down)
    depth = w_in.shape[0]
    layer_params = [_layer_params(p, l) for l in range(depth)]
    fin = norm_final.reshape(1, D_MODEL)
    n_prompt = c_prompt.shape[0]
    mod = _ada(jnp.concatenate([c_prompt, c_sample], axis=0), w_ada, b_ada)
    mod = mod.reshape(depth, mod.shape[1], 6, D_MODEL)
    mods_p = [mod[l, :n_prompt] for l in range(depth)]
    mods_s = [mod[l, n_prompt:] for l in range(depth)]
    y_prompt, p_states = _trunk(x_prompt, mods_p, _zero_states(depth, x_prompt.shape[0]), layer_params, fin)
    s_in = (state_lru_conv, state_lru_h, state_rwkv_shift, state_rwkv_S, state_s5_re, state_s5_im,
            state_ffn_conv)
    y_sample, s_states = _trunk(x_sample, mods_s, s_in, layer_params, fin)
    return (y_prompt, y_sample) + tuple(p_states) + tuple(s_states)
```

```python
import functools
import math
from typing import NamedTuple

import jax
import jax.numpy as jnp
from jax import lax
from jax.experimental import pallas as pl
from jax.experimental.pallas import tpu as pltpu

F32 = jnp.float32
BF16 = jnp.bfloat16

D_MODEL = 1024
LRU_WIDTH = 256
LRU_CONV = 4
LRU_C = 8.0
RWKV_WIDTH = 512
RWKV_HEAD = 64
RWKV_HEADS = 8
RWKV_COLS = 1792
S5_WIDTH = 256
S5_STATES = 1024
S5_POW_ROWS = 8
IN_COLS = 2560
D_FF = 2816
NORM_EPS = 1e-6
RWKV_GN_EPS = 64e-5
SUBLANES = 8
FFN_COL_BLOCK = 256
VMEM_LIMIT_BYTES = 56 * 1024 * 1024
PASSES_INTRA = 1
PASSES_INV = 3
PASSES_STATE = 1
PASSES_OUT = 1


class Tiling(NamedTuple):
    t_mix: int
    chunk: int
    group: int
    t_ffn: int


def _tiling(seq_len):
    t_mix = min(seq_len, 256)
    chunk = min(t_mix, 64)
    group = min(RWKV_HEADS, 256 // chunk)
    return Tiling(t_mix, chunk, group, min(seq_len, 512))


def _dg(a, b, ca, cb, precision=None):
    return lax.dot_general(a, b, (((ca,), (cb,)), ((), ())), precision=precision,
                           preferred_element_type=F32)


def _mm(a, w):
    return jnp.dot(a.astype(BF16), w, preferred_element_type=F32)


def _expm1(x):
    u = jnp.exp(x)
    small = u >= 0.5
    stable = (u - 1.0) * x / jnp.log(jnp.where(small, u, 0.75))
    return jnp.where(u == 1.0, x, jnp.where(small, stable, u - 1.0))


def _shift_rows(x, d, fill):
    if d % SUBLANES == 0:
        return jnp.concatenate([jnp.full((d, x.shape[1]), fill, x.dtype), x[:x.shape[0] - d]], axis=0)
    rolled = pltpu.roll(x, d, axis=0)
    if x.shape[0] <= SUBLANES:
        return jnp.where(lax.broadcasted_iota(jnp.int32, x.shape, 0) >= d, rolled, fill)
    r8 = lax.broadcasted_iota(jnp.int32, (SUBLANES, 1), 0)
    return jnp.concatenate([jnp.where(r8 >= d, rolled[0:SUBLANES], fill), rolled[SUBLANES:]], axis=0)


def _shift_in(x, d, tail):
    rolled = pltpu.roll(x, d, axis=0)
    head = rolled[0:SUBLANES]
    r8 = lax.broadcasted_iota(jnp.int32, (SUBLANES, 1), 0)
    for j in range(d):
        head = jnp.where(r8 == j, tail[SUBLANES - d + j:SUBLANES - d + j + 1], head)
    return jnp.concatenate([head, rolled[SUBLANES:]], axis=0)


def _linear_scan(a, b):
    d = 1
    while d < a.shape[0]:
        b = a * _shift_rows(b, d, 0.0) + b
        a = a * _shift_rows(a, d, 1.0)
        d *= 2
    return b


def _block_mask(rows, cols, rb, cb, dtype):
    r = lax.broadcasted_iota(jnp.int32, (rows, cols), 0) >> int(math.log2(rb))
    c = lax.broadcasted_iota(jnp.int32, (rows, cols), 1) >> int(math.log2(cb))
    return jnp.where(r == c, 1.0, 0.0).astype(dtype)


def _split(x):
    hi = x.astype(BF16)
    return hi, (x - hi.astype(F32)).astype(BF16)


def _lhs3(a, axis, passes=3):
    if passes == 1:
        return a.astype(BF16)
    hi, lo = _split(a)
    return jnp.concatenate([hi, hi, lo], axis=axis)


def _rhs3(b, axis, group=1, passes=3):
    parts = (b.astype(BF16),) if passes == 1 else _split(b)
    if group > 1:
        rows, cols = b.shape
        mask = _block_mask(group * rows, cols, rows, cols // group, BF16)
        parts = tuple(jnp.concatenate([q] * group, axis=0) * mask for q in parts)
    if passes == 1:
        return parts[0]
    return jnp.concatenate([parts[0], parts[1], parts[0]], axis=axis)


def _fold_diag(m, group):
    rows = m.shape[0] // group
    width = m.shape[1] // group
    rb = lax.broadcasted_iota(jnp.int32, m.shape, 0) >> int(math.log2(rows))
    cb = lax.broadcasted_iota(jnp.int32, m.shape, 1) >> int(math.log2(width))
    m = jnp.where(rb == cb, m, 0.0)
    out = m[0:rows]
    for h in range(1, group):
        out = out + m[h * rows:(h + 1) * rows]
    return out


def _rmsnorm_mod(x, gain, scale, shift):
    y = x * lax.rsqrt(jnp.mean(x * x, axis=-1, keepdims=True) + NORM_EPS)
    return (y * gain) * (1.0 + scale) + shift


def _ada_kernel(c_ref, w_ref, b_ref, o_ref):
    c = c_ref[...]
    o_ref[...] = _mm(c * jax.nn.sigmoid(c), w_ref[...].astype(BF16)) + b_ref[...]


def _ada(c, w_ada, b_ada):
    depth, _, n = w_ada.shape
    rows = c.shape[0]
    tn = 1536
    return pl.pallas_call(
        _ada_kernel,
        grid=(depth, n // tn),
        in_specs=[pl.BlockSpec((rows, D_MODEL), lambda l, j: (0, 0)),
                  pl.BlockSpec((None, D_MODEL, tn), lambda l, j: (l, 0, j)),
                  pl.BlockSpec((None, 1, tn), lambda l, j: (l, 0, j))],
        out_specs=pl.BlockSpec((None, rows, tn), lambda l, j: (l, 0, j)),
        out_shape=jax.ShapeDtypeStruct((depth, rows, n), F32),
        compiler_params=pltpu.CompilerParams(
            dimension_semantics=("arbitrary", "arbitrary"), vmem_limit_bytes=VMEM_LIMIT_BYTES),
        name="ada_mod",
    )(c, w_ada, b_ada.reshape(depth, 1, n))


def _s5_prep_kernel(are_ref, aim_ref, ldt_ref, bre_ref, bim_ref, bbar_re, bbar_im, pw_re, pw_im):
    dt = jnp.exp(ldt_ref[...])
    a_re = are_ref[...]
    a_im = aim_ref[...]
    mag = jnp.exp(dt * a_re)
    abr = mag * jnp.cos(dt * a_im)
    abi = mag * jnp.sin(dt * a_im)
    den = a_re * a_re + a_im * a_im
    fr = ((abr - 1.0) * a_re + abi * a_im) / den
    fi = (abi * a_re - (abr - 1.0) * a_im) / den
    bbar_re[...] = (fr * bre_ref[...] - fi * bim_ref[...]).astype(BF16)
    bbar_im[...] = (fr * bim_ref[...] + fi * bre_ref[...]).astype(BF16)
    pr, pi = abr, abi
    for k in range(S5_POW_ROWS):
        pw_re[k:k + 1, :] = pr
        pw_im[k:k + 1, :] = pi
        pr, pi = pr * pr - pi * pi, 2.0 * pr * pi


def _s5_prep(lp):
    ins = [lp["s5_a_re"], lp["s5_a_im"], lp["s5_log_dt"], lp["s5_b_re"], lp["s5_b_im"]]
    return pl.pallas_call(
        _s5_prep_kernel,
        out_shape=[jax.ShapeDtypeStruct((S5_WIDTH, S5_STATES), BF16)] * 2
        + [jax.ShapeDtypeStruct((S5_POW_ROWS, S5_STATES), F32)] * 2,
        name="s5_prep",
    )(*ins)


def _mixer_kernel(tl,
                  x_ref, mod_ref, norm_ref, w_in_ref, w_out_ref,
                  conv_w_ref, conv_b_ref, wa_ref, ba_ref, wx_ref, bx_ref, lam_ref,
                  lru_buf0_ref, lru_h0_ref,
                  mu_ref, w0_ref, w2_ref, a0_ref, a2_ref, g2_ref, kk_ref, ka_ref, rk_ref,
                  lnw_ref, lnb_ref, ones_ref, shift0_ref, s0_ref,
                  bbar_re, bbar_im, pw_re, pw_im, cre_ref, cim_ref, d_ref,
                  gluw_ref, glub_ref, s5re0_ref, s5im0_ref,
                  xo_ref, lru_buf_o, lru_h_o, shift_o, s_o, s5re_o, s5im_o):
    tc, chunk, group = tl.t_mix, tl.chunk, tl.group
    n_scan = int(math.log2(tc))
    assert n_scan <= S5_POW_ROWS
    i = pl.program_id(1)

    @pl.when(i == 0)
    def _init():
        lru_buf_o[...] = lru_buf0_ref[...]
        lru_h_o[...] = lru_h0_ref[...]
        shift_o[...] = shift0_ref[...]
        s_o[...] = s0_ref[...]
        s5re_o[...] = s5re0_ref[...]
        s5im_o[...] = s5im0_ref[...]

    x = x_ref[...]
    r8 = lax.broadcasted_iota(jnp.int32, (SUBLANES, 1), 0)
    h = _rmsnorm_mod(x, norm_ref[...], mod_ref[1:2, :], mod_ref[0:1, :])
    proj = _mm(h, w_in_ref[...])
    u_gate = proj[:, 0:256]
    u_lru = proj[:, 256:512]
    p_rw = proj[:, 512:2304]
    u_s5 = proj[:, 2304:2560]

    cw = conv_w_ref[...]
    lru_tail = lru_buf_o[...]
    xc = conv_b_ref[...] + _shift_in(u_lru, LRU_CONV - 1, lru_tail) * cw[0:1]
    for k in range(1, LRU_CONV - 1):
        xc = xc + _shift_in(u_lru, LRU_CONV - 1 - k, lru_tail) * cw[k:k + 1]
    xc = xc + u_lru * cw[LRU_CONV - 1:LRU_CONV]
    lru_buf_o[...] = u_lru[tc - SUBLANES:tc, :]
    r_gate = jax.nn.sigmoid(_mm(xc, wa_ref[...]) + ba_ref[...])
    i_gate = jax.nn.sigmoid(_mm(xc, wx_ref[...]) + bx_ref[...])
    log_a = -LRU_C * r_gate * jax.nn.softplus(-lam_ref[...])
    a = jnp.exp(log_a)
    gain = jnp.sqrt(jnp.maximum(-_expm1(2.0 * log_a), 0.0))
    b = gain * i_gate * xc
    b = jnp.concatenate([b[0:SUBLANES] + jnp.where(r8 == 0, a[0:SUBLANES] * lru_h_o[...], 0.0),
                         b[SUBLANES:]], axis=0)
    h_lru = _linear_scan(a, b)
    lru_h_o[...] = h_lru[tc - 1:tc, :]
    y_a = h_lru * jax.nn.gelu(u_gate)

    rolled = pltpu.roll(p_rw, 1, axis=0)
    prev =jnp.concatenate([jnp.where(r8 == 0, shift_o[...], rolled[0:SUBLANES]), rolled[SUBLANES:]], axis=0)
    shift_o[...] = p_rw[tc - 1:tc, :]
    xm = p_rw + (prev - p_rw) * mu_ref[...]
    r_t = xm[:, 0:512]
    k_t = xm[:, 512:1024]
    v_t = xm[:, 1024:1536]
    lo = xm[:, 1536:1664]
    g_lo = xm[:, 1664:1792]
    ones2 = ones_ref[...]

    def head_sum(z):
        hi, lo = _split(z)
        return _dg(jnp.concatenate([hi, lo], axis=1), ones2, 1, 0)

    log_w = -jax.nn.softplus(-(w0_ref[...] + _mm(jnp.tanh(lo), w2_ref[...]))) - 0.5
    a_t = jax.nn.sigmoid(a0_ref[...] + _mm(lo, a2_ref[...]))
    g_t = _mm(jax.nn.sigmoid(g_lo), g2_ref[...])
    kk = k_t * kk_ref[...]
    kk = kk * lax.rsqrt(jnp.maximum(head_sum(kk * kk), 1e-24))
    kp = k_t * (1.0 + (a_t - 1.0) * ka_ref[...])
    bq_t = kk * a_t
    ld_t = -jnp.exp(log_w)

    gw = group * RWKV_HEAD
    crow = lax.broadcasted_iota(jnp.int32, (chunk, group * chunk), 0)
    ccol = lax.broadcasted_iota(jnp.int32, (chunk, group * chunk), 1) & (chunk - 1)
    strict = ccol < crow
    incl = ccol <= crow
    diag = ccol == crow
    n_pow = int(math.log2(chunk))

    n_chunks = tc // chunk
    n_groups = RWKV_HEADS // group
    lanes = [slice(g * gw, (g + 1) * gw) for g in range(n_groups)]
    chains = [(c, g) for c in range(n_chunks) for g in range(n_groups)]

    a3, a3s, bb3, kb3, v3, kgbg3, v_c, g_end_c = {}, {}, {}, {}, {}, {}, {}, {}
    for c in range(n_chunks):
        sl = slice(c * chunk, (c + 1) * chunk)
        r, kpc, v, kkc, bq, ld = r_t[sl], kp[sl], v_t[sl], kk[sl], bq_t[sl], ld_t[sl]
        cum = ld
        d = 1
        while d < chunk:
            cum = cum + _shift_rows(cum, d, 0.0)
            d *= 2
        cum_end = cum[chunk - 1:chunk, :]
        inv = jnp.exp(-cum)
        tail_decay = jnp.exp(cum_end - cum)
        rt = r * jnp.exp(cum)
        kkt = kkc * jnp.exp(cum - ld)
        kb = kpc * inv
        bb = bq * inv
        kg = kpc * tail_decay
        bg = bq * tail_decay
        g_end_c[c] = jnp.exp(cum_end)
        for g, ls in enumerate(lanes):
            a_in = jnp.concatenate([kkt[:, ls], rt[:, ls]], axis=0)
            a3[c, g] = _lhs3(a_in, 1, PASSES_INTRA)
            a3s[c, g] = a3[c, g] if PASSES_STATE == PASSES_INTRA else _lhs3(a_in, 1, PASSES_STATE)
            bb3[c, g] = _rhs3(bb[:, ls], 1, group, PASSES_INTRA)
            kb3[c, g] = _rhs3(kb[:, ls], 1, group, PASSES_INTRA)
            v3[c, g] = _rhs3(v[:, ls], 0, group, PASSES_INTRA)
            kgbg3[c, g] = _rhs3(jnp.concatenate([kg[:, ls], -bg[:, ls]], axis=0), 0, 1, PASSES_STATE)
            v_c[c, g] = v[:, ls]

    xb = {ch: _dg(a3[ch], bb3[ch], 1, 1) for ch in chains}
    xk = {ch: _dg(a3[ch], kb3[ch], 1, 1) for ch in chains}
    n = {ch: jnp.where(strict, -xb[ch][0:chunk], 0.0) for ch in chains}
    rb3 = {ch: _lhs3(jnp.where(incl, xb[ch][chunk:], 0.0), 1, PASSES_OUT) for ch in chains}
    lv = {ch: _dg(_lhs3(jnp.concatenate([jnp.where(strict, xk[ch][0:chunk], 0.0),
                                          jnp.where(incl, xk[ch][chunk:], 0.0)], axis=0), 1, PASSES_INTRA),
                  v3[ch], 1, 0) for ch in chains}
    t = {ch: jnp.where(diag, 1.0, n[ch]) for ch in chains}
    p = {ch: _dg(_lhs3(n[ch], 1, PASSES_INV), _rhs3(n[ch], 0, group, PASSES_INV), 1, 0) for ch in chains}
    for q in range(1, n_pow):
        last = q + 1 == n_pow
        tp = {ch: _dg(_lhs3(t[ch] if last else jnp.concatenate([t[ch], p[ch]], axis=0), 1, PASSES_INV),
                      _rhs3(p[ch], 0, group, PASSES_INV), 1, 0) for ch in chains}
        t = {ch: t[ch] + tp[ch][0:chunk] for ch in chains}
        if not last:
            p = {ch: tp[ch][chunk:] for ch in chains}
    t3 = {ch: _lhs3(t[ch], 1, PASSES_STATE) for ch in chains}

    y_rows = []
    for c in range(n_chunks):
        s_prev = [s_o[:, ls] for ls in lanes]
        xs = [_dg(a3s[c, g], _rhs3(s_prev[g], 1, group, PASSES_STATE), 1, 1) for g in range(n_groups)]
        u = [_dg(t3[c, g], _rhs3(xs[g][0:chunk] + lv[c, g][0:chunk], 0, group, PASSES_STATE), 1, 0)
             for g in range(n_groups)]
        ru = [_dg(rb3[c, g], _rhs3(u[g], 0, group, PASSES_OUT), 1, 0) for g in range(n_groups)]
        upd = [_dg(_lhs3(jnp.concatenate([v_c[c, g], u[g]], axis=0), 0, PASSES_STATE), kgbg3[c, g], 0, 0)
               for g in range(n_groups)]
        y_rows.append(jnp.concatenate(
            [xs[g][chunk:] + lv[c, g][chunk:] - ru[g] for g in range(n_groups)], axis=1))
        for g, ls in enumerate(lanes):
            s_o[:, ls] = s_prev[g] * g_end_c[c][:, ls] + _fold_diag(upd[g], group)
    y = jnp.concatenate(y_rows, axis=0)
    inv_n = 1.0 / RWKV_HEAD
    mean = head_sum(y) * inv_n
    yc = y - mean
    var = head_sum(yc * yc) * inv_n
    yn = yc * lax.rsqrt(var + RWKV_GN_EPS) * lnw_ref[...] + lnb_ref[...]
    bonus = head_sum(r_t * kp * rk_ref[...]) * v_t
    y_b = (yn + bonus) * g_t

    bu_re = _mm(u_s5, bbar_re[...])
    bu_im = _mm(u_s5, bbar_im[...])
    h0r = s5re_o[...]
    h0i = s5im_o[...]
    ar = pw_re[0:1, :]
    ai = pw_im[0:1, :]
    first = r8 == 0
    hr = jnp.concatenate([bu_re[0:SUBLANES] + jnp.where(first, ar * h0r - ai * h0i, 0.0),
                          bu_re[SUBLANES:]], axis=0)
    hi = jnp.concatenate([bu_im[0:SUBLANES] + jnp.where(first, ar * h0i + ai * h0r, 0.0),
                          bu_im[SUBLANES:]], axis=0)
    d = 1
    for k in range(n_scan):
        pr = pw_re[k:k + 1, :]
        pi = pw_im[k:k + 1, :]
        sr = _shift_rows(hr, d, 0.0)
        si = _shift_rows(hi, d, 0.0)
        hr, hi = hr + (pr * sr - pi * si), hi + (pr * si + pi * sr)
        d *= 2
    s5re_o[...] = hr[tc - 1:tc, :]
    s5im_o[...] = hi[tc - 1:tc, :]
    y5 = _mm(hr, cre_ref[...]) - _mm(hi, cim_ref[...]) + d_ref[...] * u_s5
    z = jax.nn.gelu(y5)
    y_c = z * jax.nn.sigmoid(_mm(z, gluw_ref[...]) + glub_ref[...])

    mix = _mm(jnp.concatenate([y_a, y_b, y_c], axis=-1), w_out_ref[...])
    xo_ref[...] = x + mod_ref[2:3, :] * mix


def _const_spec(arr):
    nd = arr.ndim
    return pl.BlockSpec(arr.shape, lambda b, i: (0,) * nd)


def _batch_spec(arr):
    nd = arr.ndim
    return pl.BlockSpec((None,) + arr.shape[1:], lambda b, i: (b,) + (0,) * (nd - 1))


def _mixer(tl, x, mod, lp, st):
    bsz, seq, _ = x.shape
    tc = tl.t_mix
    consts1 = [lp["norm_mix"], lp["w_in"], lp["w_out"],
               lp["lru_conv_w"], lp["lru_conv_b"], lp["lru_wa"], lp["lru_ba"], lp["lru_wx"],
               lp["lru_bx"], lp["lru_lambda"]]
    states1 = [st["lru_buf"], st["lru_h"]]
    consts2 = [lp["rwkv_mu"], lp["rwkv_w0"], lp["rwkv_w2"], lp["rwkv_a0"], lp["rwkv_a2"],
               lp["rwkv_g2"], lp["rwkv_k_k"], lp["rwkv_k_a"], lp["rwkv_r_k"], lp["rwkv_ln_w"],
               lp["rwkv_ln_b"], lp["ones_bd"]]
    states2 = [st["rwkv_shift"], st["rwkv_s"]]
    consts3 = [lp["s5_bbar_re"], lp["s5_bbar_im"], lp["s5_pow_re"], lp["s5_pow_im"],
               lp["s5_c_re"], lp["s5_c_im"], lp["s5_d"], lp["s5_glu_w"], lp["s5_glu_b"]]
    states3 = [st["s5_re"], st["s5_im"]]
    x_spec = pl.BlockSpec((None, tc, D_MODEL), lambda b, i: (b, i, 0))
    in_specs = ([x_spec, _batch_spec(mod)]
                + [_const_spec(a) for a in consts1] + [_batch_spec(a) for a in states1]
                + [_const_spec(a) for a in consts2] + [_batch_spec(a) for a in states2]
                + [_const_spec(a) for a in consts3] + [_batch_spec(a) for a in states3])
    state_list = states1 + states2 + states3
    out_shape = [jax.ShapeDtypeStruct(x.shape, F32)] + [jax.ShapeDtypeStruct(a.shape, F32) for a in state_list]
    out_specs = [x_spec] + [_batch_spec(a) for a in state_list]
    outs = pl.pallas_call(
        functools.partial(_mixer_kernel, tl),
        grid=(bsz, seq // tc),
        in_specs=in_specs, out_specs=out_specs, out_shape=out_shape,
        compiler_params=pltpu.CompilerParams(
            dimension_semantics=("arbitrary", "arbitrary"), vmem_limit_bytes=VMEM_LIMIT_BYTES),
        name="mixer",
    )(x, mod, *consts1, *states1, *consts2, *states2, *consts3, *states3)
    x_new = outs[0]
    keys = ["lru_buf", "lru_h", "rwkv_shift", "rwkv_s", "s5_re", "s5_im"]
    return x_new, dict(zip(keys, outs[1:]))


def _ffn_kernel(tl, final,
                x_ref, mod_ref, norm_ref, up_ref, cw_ref, cb_ref, down_ref, tail0_ref, fin_ref,
                xo_ref, tail_o):
    tf = tl.t_ffn
    i = pl.program_id(1)

    @pl.when(i == 0)
    def _init():
        tail_o[...] = tail0_ref[...]

    x = x_ref[...]
    h = _rmsnorm_mod(x, norm_ref[...], mod_ref[4:5, :], mod_ref[3:4, :]).astype(BF16)

    def conv_block(c0):
        cs = slice(c0, c0 + FFN_COL_BLOCK)
        up = jnp.dot(h, up_ref[:, cs], preferred_element_type=F32)
        tail = tail_o[:, cs]
        m1 = _shift_in(up, 1, tail)
        m2 = _shift_in(up, 2, tail)
        tail_o[:, cs] = up[tf - SUBLANES:tf, :]
        return cb_ref[:, cs] + m2 * cw_ref[0:1, cs] + m1 * cw_ref[1:2, cs] + up * cw_ref[2:3, cs]

    acc = jnp.zeros((tf, D_MODEL), F32)
    for c0 in range(0, D_FF, FFN_COL_BLOCK):
        val = conv_block(c0)
        gate = conv_block(D_FF + c0)
        act = (val * (gate * jax.nn.sigmoid(gate))).astype(BF16)
        acc = acc + jnp.dot(act, down_ref[c0:c0 + FFN_COL_BLOCK, :], preferred_element_type=F32)
    y = x + mod_ref[5:6, :] * acc
    if final:
        y = y * lax.rsqrt(jnp.mean(y * y, axis=-1, keepdims=True) + NORM_EPS) * fin_ref[...]
    xo_ref[...] = y


def _ffn(tl, final, x, mod, lp, tail0, norm_final):
    bsz, seq, _ = x.shape
    tf = tl.t_ffn
    consts = [lp["norm_ffn"], lp["ffn_up"], lp["ffn_conv_w"], lp["ffn_conv_b"], lp["ffn_down"]]
    x_spec = pl.BlockSpec((None, tf, D_MODEL), lambda b, i: (b, i, 0))
    in_specs = ([x_spec, _batch_spec(mod)] + [_const_spec(a) for a in consts]
                + [_batch_spec(tail0), _const_spec(norm_final)])
    x_new, tail = pl.pallas_call(
        functools.partial(_ffn_kernel, tl, final),
        grid=(bsz, seq // tf),
        in_specs=in_specs,
        out_specs=[x_spec, _batch_spec(tail0)],
        out_shape=[jax.ShapeDtypeStruct(x.shape, F32), jax.ShapeDtypeStruct(tail0.shape, F32)],
        compiler_params=pltpu.CompilerParams(
            dimension_semantics=("arbitrary", "arbitrary"), vmem_limit_bytes=VMEM_LIMIT_BYTES),
        name="conv_ffn",
    )(x, mod, *consts, tail0, norm_final)
    return x_new, tail


def _block_diag_dense(blocks):
    n, r, c = blocks.shape
    eye = jnp.eye(n, dtype=blocks.dtype)
    return (eye[:, None, :, None] * blocks[:, :, None, :]).reshape(n * r, n * c)


def _layer_params(p, l):
    row = lambda a: a.reshape(1, -1)
    zeros_lora = jnp.zeros((64, RWKV_WIDTH), F32)
    head = jnp.arange(RWKV_WIDTH) // RWKV_HEAD
    return {
        "norm_mix": row(p["norm_mix"][l]), "norm_ffn": row(p["norm_ffn"][l]),
        "w_in": p["w_in"][l].astype(BF16), "w_out": p["w_out"][l].astype(BF16),
        "lru_conv_w": p["lru_conv_w"][l], "lru_conv_b": row(p["lru_conv_b"][l]),
        "lru_wa": _block_diag_dense(p["lru_wa"][l]).astype(BF16), "lru_ba": row(p["lru_ba"][l]),
        "lru_wx": _block_diag_dense(p["lru_wx"][l]).astype(BF16), "lru_bx": row(p["lru_bx"][l]),
        "lru_lambda": row(p["lru_lambda"][l]),
        "rwkv_mu": row(p["rwkv_mu"][l]), "rwkv_w0": row(p["rwkv_w0"][l]),
        "rwkv_w2": jnp.concatenate([p["rwkv_w2"][l], zeros_lora], axis=0).astype(BF16),
        "rwkv_a0": row(p["rwkv_a0"][l]),
        "rwkv_a2": jnp.concatenate([zeros_lora, p["rwkv_a2"][l]], axis=0).astype(BF16),
        "rwkv_g2": p["rwkv_g2"][l].astype(BF16),
        "rwkv_k_k": row(p["rwkv_k_k"][l]), "rwkv_k_a": row(p["rwkv_k_a"][l]),
        "rwkv_r_k": row(p["rwkv_r_k"][l]), "rwkv_ln_w": row(p["rwkv_ln_w"][l]),
        "rwkv_ln_b": row(p["rwkv_ln_b"][l]),
        "ones_bd": jnp.tile(head[:, None] == head[None, :], (2, 1)).astype(BF16),
        "s5_a_re": row(p["s5_a_re"][l]), "s5_a_im": row(p["s5_a_im"][l]),
        "s5_log_dt": row(jnp.repeat(p["s5_log_dt"][l], 64)),
        "s5_b_re": _block_diag_dense(jnp.swapaxes(p["s5_b_re"][l], 1, 2)),
        "s5_b_im": _block_diag_dense(jnp.swapaxes(p["s5_b_im"][l], 1, 2)),
        "s5_c_re": _block_diag_dense(jnp.swapaxes(p["s5_c_re"][l], 1, 2)).astype(BF16),
        "s5_c_im": _block_diag_dense(jnp.swapaxes(p["s5_c_im"][l], 1, 2)).astype(BF16),
        "s5_d": row(p["s5_d"][l]),
        "s5_glu_w": p["s5_glu_w"][l].astype(BF16), "s5_glu_b": row(p["s5_glu_b"][l]),
        "ffn_up": p["ffn_up"][l].astype(BF16), "ffn_conv_w": p["ffn_conv_w"][l],
        "ffn_conv_b": row(p["ffn_conv_b"][l]), "ffn_down": p["ffn_down"][l].astype(BF16),
    }


def _pad_tail(buf):
    return jnp.pad(buf, ((0, 0), (SUBLANES - buf.shape[1], 0), (0, 0)))


def _state_in(states, l):
    lru_buf, lru_h, rw_shift, rw_s, s5_re, s5_im, ffn_buf = (s[l] for s in states)
    bsz = lru_h.shape[0]
    return {
        "lru_buf": _pad_tail(lru_buf),
        "lru_h": lru_h.reshape(bsz, 1, LRU_WIDTH),
        "rwkv_shift": rw_shift.reshape(bsz, 1, RWKV_COLS),
        "rwkv_s": jnp.transpose(rw_s, (0, 2, 1, 3)).reshape(bsz, RWKV_HEAD, RWKV_WIDTH),
        "s5_re": s5_re.reshape(bsz, 1, S5_STATES),
        "s5_im": s5_im.reshape(bsz, 1, S5_STATES),
    }, _pad_tail(ffn_buf)


def _state_out(st, ffn_tail):
    bsz = st["lru_h"].shape[0]
    return (st["lru_buf"][:, SUBLANES - (LRU_CONV - 1):, :],
            st["lru_h"].reshape(bsz, LRU_WIDTH),
            st["rwkv_shift"].reshape(bsz, RWKV_COLS),
            jnp.transpose(st["rwkv_s"].reshape(bsz, RWKV_HEAD, RWKV_HEADS, RWKV_HEAD), (0, 2, 1, 3)),
            st["s5_re"].reshape(bsz, 16, 64),
            st["s5_im"].reshape(bsz, 16, 64),
            ffn_tail[:, SUBLANES - 2:, :])


def _trunk(x, mods, states, layer_params, norm_final):
    depth = len(layer_params)
    tl = _tiling(x.shape[1])
    new = []
    for l in range(depth):
        st, ffn_tail0 = _state_in(states, l)
        x, st = _mixer(tl, x, mods[l], layer_params[l], st)
        x, ffn_tail = _ffn(tl, l == depth - 1, x, mods[l], layer_params[l], ffn_tail0, norm_final)
        new.append(_state_out(st, ffn_tail))
    stacked = tuple(jnp.stack([n[j] for n in new], axis=0) for j in range(7))
    return x, stacked


def _zero_states(depth, bsz):
    return (jnp.zeros((depth, bsz, LRU_CONV - 1, LRU_WIDTH), F32),
            jnp.zeros((depth, bsz, LRU_WIDTH), F32),
            jnp.zeros((depth, bsz, RWKV_COLS), F32),
            jnp.zeros((depth, bsz, RWKV_HEADS, RWKV_HEAD, RWKV_HEAD), F32),
            jnp.zeros((depth, bsz, 16, 64), F32),
            jnp.zeros((depth, bsz, 16, 64), F32),
            jnp.zeros((depth, bsz, 2, 2 * D_FF), F32))


def kernel(x_prompt, x_sample, c_prompt, c_sample, state_lru_conv, state_lru_h, state_rwkv_shift, state_rwkv_S, state_s5_re, state_s5_im, state_ffn_conv, w_ada, b_ada, norm_mix, norm_ffn, w_in, w_out, lru_conv_w, lru_conv_b, lru_wa, lru_ba, lru_wx, lru_bx, lru_lambda, rwkv_mu, rwkv_w0, rwkv_w2, rwkv_a0, rwkv_a2, rwkv_g2, rwkv_k_k, rwkv_k_a, rwkv_r_k, rwkv_ln_w, rwkv_ln_b, s5_a_re, s5_a_im, s5_b_re, s5_b_im, s5_c_re, s5_c_im, s5_d, s5_log_dt, s5_glu_w, s5_glu_b, ffn_up, ffn_conv_w, ffn_conv_b, ffn_down, norm_final):
    p = dict(norm_mix=norm_mix, norm_ffn=norm_ffn, w_in=w_in, w_out=w_out,
             lru_conv_w=lru_conv_w, lru_conv_b=lru_conv_b, lru_wa=lru_wa, lru_ba=lru_ba,
             lru_wx=lru_wx, lru_bx=lru_bx, lru_lambda=lru_lambda,
             rwkv_mu=rwkv_mu, rwkv_w0=rwkv_w0, rwkv_w2=rwkv_w2, rwkv_a0=rwkv_a0, rwkv_a2=rwkv_a2,
             rwkv_g2=rwkv_g2, rwkv_k_k=rwkv_k_k, rwkv_k_a=rwkv_k_a,
             rwkv_r_k=rwkv_r_k.reshape(rwkv_r_k.shape[0], RWKV_WIDTH),
             rwkv_ln_w=rwkv_ln_w, rwkv_ln_b=rwkv_ln_b,
             s5_a_re=s5_a_re, s5_a_im=s5_a_im, s5_b_re=s5_b_re, s5_b_im=s5_b_im,
             s5_c_re=s5_c_re, s5_c_im=s5_c_im, s5_d=s5_d, s5_log_dt=s5_log_dt,
             s5_glu_w=s5_glu_w, s5_glu_b=s5_glu_b,
             ffn_up=ffn_up, ffn_conv_w=ffn_conv_w, ffn_conv_b=ffn_conv_b, ffn_down=ffn_down)
    depth = w_in.shape[0]
    layer_params = [_layer_params(p, l) for l in range(depth)]
    for lp in layer_params:
        lp["s5_bbar_re"], lp["s5_bbar_im"], lp["s5_pow_re"], lp["s5_pow_im"] = _s5_prep(lp)
    fin = norm_final.reshape(1, D_MODEL)
    n_prompt = c_prompt.shape[0]
    mod = _ada(jnp.concatenate([c_prompt, c_sample], axis=0), w_ada, b_ada)
    mod = mod.reshape(depth, mod.shape[1], 6, D_MODEL)
    mods_p = [mod[l, :n_prompt] for l in range(depth)]
    mods_s = [mod[l, n_prompt:] for l in range(depth)]
    s_in = (state_lru_conv, state_lru_h, state_rwkv_shift, state_rwkv_S, state_s5_re, state_s5_im,
            state_ffn_conv)
    y_sample, s_states = _trunk(x_sample, mods_s, s_in, layer_params, fin)
    y_prompt, p_states = _trunk(x_prompt, mods_p, _zero_states(depth, x_prompt.shape[0]), layer_params, fin)
    return (y_prompt, y_sample) + tuple(p_states) + tuple(s_states)
```

```python
import functools
import math
from typing import NamedTuple

import jax
import jax.numpy as jnp
from jax import lax
from jax.experimental import pallas as pl
from jax.experimental.pallas import tpu as pltpu

F32 = jnp.float32
BF16 = jnp.bfloat16

D_MODEL = 1024
LRU_WIDTH = 256
LRU_CONV = 4
LRU_C = 8.0
RWKV_WIDTH = 512
RWKV_HEAD = 64
RWKV_HEADS = 8
RWKV_COLS = 1792
S5_WIDTH = 256
S5_STATES = 1024
S5_POW_ROWS = 8
S5_SUB = 16
IN_COLS = 2560
D_FF = 2816
NORM_EPS = 1e-6
RWKV_GN_EPS = 64e-5
SUBLANES = 8
FFN_COL_BLOCK = 256
FFN_DOWN_GROUP = 4
VMEM_LIMIT_BYTES = 56 * 1024 * 1024
PASSES_INTRA = 1
PASSES_INV = 3
INV_ACCURATE_ROUNDS = 3
PASSES_STATE = 1
PASSES_OUT = 1


class Tiling(NamedTuple):
    t_mix: int
    chunk: int
    group: int
    t_ffn: int


def _tiling(seq_len):
    t_mix = min(seq_len, 256)
    chunk = min(t_mix, 64)
    group = min(RWKV_HEADS, 256 // chunk)
    return Tiling(t_mix, chunk, group, min(seq_len, 512))


def _dg(a, b, ca, cb, precision=None):
    return lax.dot_general(a, b, (((ca,), (cb,)), ((), ())), precision=precision,
                           preferred_element_type=F32)


def _mm(a, w):
    return jnp.dot(a.astype(BF16), w, preferred_element_type=F32)


def _expm1(x):
    u = jnp.exp(x)
    small = u >= 0.5
    stable = (u - 1.0) * x / jnp.log(jnp.where(small, u, 0.75))
    return jnp.where(u == 1.0, x, jnp.where(small, stable, u - 1.0))


def _shift_rows(x, d, fill):
    if d % SUBLANES == 0:
        return jnp.concatenate([jnp.full((d, x.shape[1]), fill, x.dtype), x[:x.shape[0] - d]], axis=0)
    rolled = pltpu.roll(x, d, axis=0)
    if x.shape[0] <= SUBLANES:
        return jnp.where(lax.broadcasted_iota(jnp.int32, x.shape, 0) >= d, rolled, fill)
    r8 = lax.broadcasted_iota(jnp.int32, (SUBLANES, 1), 0)
    return jnp.concatenate([jnp.where(r8 >= d, rolled[0:SUBLANES], fill), rolled[SUBLANES:]], axis=0)


def _shift_in(x, d, tail):
    rolled = pltpu.roll(x, d, axis=0)
    head = rolled[0:SUBLANES]
    r8 = lax.broadcasted_iota(jnp.int32, (SUBLANES, 1), 0)
    for j in range(d):
        head = jnp.where(r8 == j, tail[SUBLANES - d + j:SUBLANES - d + j + 1], head)
    return jnp.concatenate([head, rolled[SUBLANES:]], axis=0)


def _linear_scan(a, b):
    d = 1
    while d < a.shape[0]:
        b = a * _shift_rows(b, d, 0.0) + b
        a = a * _shift_rows(a, d, 1.0)
        d *= 2
    return b


def _block_mask(rows, cols, rb, cb, dtype):
    r = lax.broadcasted_iota(jnp.int32, (rows, cols), 0) >> int(math.log2(rb))
    c = lax.broadcasted_iota(jnp.int32, (rows, cols), 1) >> int(math.log2(cb))
    return jnp.where(r == c, 1.0, 0.0).astype(dtype)


def _split(x):
    hi = x.astype(BF16)
    return hi, (x - hi.astype(F32)).astype(BF16)


def _lhs3(a, axis, passes=3):
    if passes == 1:
        return a.astype(BF16)
    hi, lo = _split(a)
    return jnp.concatenate([hi, hi, lo], axis=axis)


def _rhs3(b, axis, group=1, passes=3):
    parts = (b.astype(BF16),) if passes == 1 else _split(b)
    if group > 1:
        rows, cols = b.shape
        mask = _block_mask(group * rows, cols, rows, cols // group, BF16)
        parts = tuple(jnp.concatenate([q] * group, axis=0) * mask for q in parts)
    if passes == 1:
        return parts[0]
    return jnp.concatenate([parts[0], parts[1], parts[0]], axis=axis)


def _fold_diag(m, group):
    rows = m.shape[0] // group
    width = m.shape[1] // group
    rb = lax.broadcasted_iota(jnp.int32, m.shape, 0) >> int(math.log2(rows))
    cb = lax.broadcasted_iota(jnp.int32, m.shape, 1) >> int(math.log2(width))
    m = jnp.where(rb == cb, m, 0.0)
    out = m[0:rows]
    for h in range(1, group):
        out = out + m[h * rows:(h + 1) * rows]
    return out


def _rmsnorm_mod(x, gain, scale, shift):
    y = x * lax.rsqrt(jnp.mean(x * x, axis=-1, keepdims=True) + NORM_EPS)
    return (y * gain) * (1.0 + scale) + shift


def _ada_kernel(c_ref, w_ref, b_ref, o_ref):
    c = c_ref[...]
    o_ref[...] = _mm(c * jax.nn.sigmoid(c), w_ref[...].astype(BF16)) + b_ref[...]


def _ada(c, w_ada, b_ada):
    depth, _, n = w_ada.shape
    rows = c.shape[0]
    tn = 1536
    return pl.pallas_call(
        _ada_kernel,
        grid=(depth, n // tn),
        in_specs=[pl.BlockSpec((rows, D_MODEL), lambda l, j: (0, 0)),
                  pl.BlockSpec((None, D_MODEL, tn), lambda l, j: (l, 0, j)),
                  pl.BlockSpec((None, 1, tn), lambda l, j: (l, 0, j))],
        out_specs=pl.BlockSpec((None, rows, tn), lambda l, j: (l, 0, j)),
        out_shape=jax.ShapeDtypeStruct((depth, rows, n), F32),
        compiler_params=pltpu.CompilerParams(
            dimension_semantics=("arbitrary", "arbitrary"), vmem_limit_bytes=VMEM_LIMIT_BYTES),
        name="ada_mod",
    )(c, w_ada, b_ada.reshape(depth, 1, n))


def _s5_prep_kernel(are_ref, aim_ref, ldt_ref, bre_ref, bim_ref, cre_ref, cim_ref,
                    bbar_re, bbar_im, pw_re, pw_im, lagk, tab_end_re, tab_end_im, tab_in_re, tab_in_im):
    dt = jnp.exp(ldt_ref[...])
    a_re = are_ref[...]
    a_im = aim_ref[...]
    mag = jnp.exp(dt * a_re)
    abr = mag * jnp.cos(dt * a_im)
    abi = mag * jnp.sin(dt * a_im)
    den = a_re * a_re + a_im * a_im
    fr = ((abr - 1.0) * a_re + abi * a_im) / den
    fi = (abi * a_re - (abr - 1.0) * a_im) / den
    bb_re = fr * bre_ref[...] - fi * bim_ref[...]
    bb_im = fr * bim_ref[...] + fi * bre_ref[...]
    bbar_re[...] = bb_re.astype(BF16)
    bbar_im[...] = bb_im.astype(BF16)
    pr, pi = abr, abi
    for k in range(S5_POW_ROWS):
        pw_re[k:k + 1, :] = pr
        pw_im[k:k + 1, :] = pi
        pr, pi = pr * pr - pi * pi, 2.0 * pr * pi
    c3_re = _rhs3(cre_ref[...], 0)
    c3_im = _rhs3(cim_ref[...], 0)
    pr, pi = jnp.ones_like(abr), jnp.zeros_like(abi)
    for d in range(S5_SUB + 1):
        if d < S5_SUB:
            lagk[d * S5_WIDTH:(d + 1) * S5_WIDTH, :] = (
                _dg(_lhs3(bb_re * pr - bb_im * pi, 1), c3_re, 1, 0)
                - _dg(_lhs3(bb_re * pi + bb_im * pr, 1), c3_im, 1, 0)).astype(BF16)
            tab_end_re[S5_SUB - 1 - d:S5_SUB - d, :] = pr
            tab_end_im[S5_SUB - 1 - d:S5_SUB - d, :] = pi
        if d > 0:
            tab_in_re[d - 1:d, :] = pr
            tab_in_im[d - 1:d, :] = pi
        pr, pi = pr * abr - pi * abi, pr * abi + pi * abr


def _s5_prep(lp):
    ins = [lp["s5_a_re"], lp["s5_a_im"], lp["s5_log_dt"], lp["s5_b_re"], lp["s5_b_im"],
           lp["s5_c_re_f32"], lp["s5_c_im_f32"]]
    return pl.pallas_call(
        _s5_prep_kernel,
        out_shape=[jax.ShapeDtypeStruct((S5_WIDTH, S5_STATES), BF16)] * 2
        + [jax.ShapeDtypeStruct((S5_POW_ROWS, S5_STATES), F32)] * 2
        + [jax.ShapeDtypeStruct((S5_SUB * S5_WIDTH, S5_WIDTH), BF16)]
        + [jax.ShapeDtypeStruct((S5_SUB, S5_STATES), F32)] * 4,
        compiler_params=pltpu.CompilerParams(vmem_limit_bytes=VMEM_LIMIT_BYTES),
        name="s5_prep",
    )(*ins)


def _mixer_kernel(tl,
                  x_ref, mod_ref, norm_ref, w_in_ref, w_out_ref,
                  conv_w_ref, conv_b_ref, wa_ref, ba_ref, wx_ref, bx_ref, lam_ref,
                  lru_buf0_ref, lru_h0_ref,
                  mu_ref, w0_ref, w2_ref, a0_ref, a2_ref, g2_ref, kk_ref, ka_ref, rk_ref,
                  lnw_ref, lnb_ref, ones_ref, shift0_ref, s0_ref,
                  bbar_re, bbar_im, pw_re, pw_im, lagk_ref, tab_end_re, tab_end_im, tab_in_re, tab_in_im,
                  lag_shift_ref, sub_sum_ref, cre_ref, cim_ref, d_ref,
                  gluw_ref, glub_ref, s5re0_ref, s5im0_ref,
                  xo_ref, lru_buf_o, lru_h_o, shift_o, s_o, s5re_o, s5im_o):
    tc, chunk, group = tl.t_mix, tl.chunk, tl.group
    n_scan = int(math.log2(tc))
    assert n_scan <= S5_POW_ROWS
    i = pl.program_id(1)

    @pl.when(i == 0)
    def _init():
        lru_buf_o[...] = lru_buf0_ref[...]
        lru_h_o[...] = lru_h0_ref[...]
        shift_o[...] = shift0_ref[...]
        s_o[...] = s0_ref[...]
        s5re_o[...] = s5re0_ref[...]
        s5im_o[...] = s5im0_ref[...]

    x = x_ref[...]
    r8 = lax.broadcasted_iota(jnp.int32, (SUBLANES, 1), 0)
    h = _rmsnorm_mod(x, norm_ref[...], mod_ref[1:2, :], mod_ref[0:1, :])
    proj = _mm(h, w_in_ref[...])
    u_gate = proj[:, 0:256]
    u_lru = proj[:, 256:512]
    p_rw = proj[:, 512:2304]
    u_s5 = proj[:, 2304:2560]

    cw = conv_w_ref[...]
    lru_tail = lru_buf_o[...]
    xc = conv_b_ref[...] + _shift_in(u_lru, LRU_CONV - 1, lru_tail) * cw[0:1]
    for k in range(1, LRU_CONV - 1):
        xc = xc + _shift_in(u_lru, LRU_CONV - 1 - k, lru_tail) * cw[k:k + 1]
    xc = xc + u_lru * cw[LRU_CONV - 1:LRU_CONV]
    lru_buf_o[...] = u_lru[tc - SUBLANES:tc, :]
    r_gate = jax.nn.sigmoid(_mm(xc, wa_ref[...]) + ba_ref[...])
    i_gate = jax.nn.sigmoid(_mm(xc, wx_ref[...]) + bx_ref[...])
    log_a = -LRU_C * r_gate * jax.nn.softplus(-lam_ref[...])
    a = jnp.exp(log_a)
    gain = jnp.sqrt(jnp.maximum(-_expm1(2.0 * log_a), 0.0))
    b = gain * i_gate * xc
    b = jnp.concatenate([b[0:SUBLANES] + jnp.where(r8 == 0, a[0:SUBLANES] * lru_h_o[...], 0.0),
                         b[SUBLANES:]], axis=0)
    h_lru = _linear_scan(a, b)
    lru_h_o[...] = h_lru[tc - 1:tc, :]
    y_a = h_lru * jax.nn.gelu(u_gate)

    rolled = pltpu.roll(p_rw, 1, axis=0)
    prev =jnp.concatenate([jnp.where(r8 == 0, shift_o[...], rolled[0:SUBLANES]), rolled[SUBLANES:]], axis=0)
    shift_o[...] = p_rw[tc - 1:tc, :]
    xm = p_rw + (prev - p_rw) * mu_ref[...]
    r_t = xm[:, 0:512]
    k_t = xm[:, 512:1024]
    v_t = xm[:, 1024:1536]
    lo = xm[:, 1536:1664]
    g_lo = xm[:, 1664:1792]
    ones2 = ones_ref[...]

    def head_sum(z):
        hi, lo = _split(z)
        return _dg(jnp.concatenate([hi, lo], axis=1), ones2, 1, 0)

    log_w = -jax.nn.softplus(-(w0_ref[...] + _mm(jnp.tanh(lo), w2_ref[...]))) - 0.5
    a_t = jax.nn.sigmoid(a0_ref[...] + _mm(lo, a2_ref[...]))
    g_t = _mm(jax.nn.sigmoid(g_lo), g2_ref[...])
    kk = k_t * kk_ref[...]
    kk = kk * lax.rsqrt(jnp.maximum(head_sum(kk * kk), 1e-24))
    kp = k_t * (1.0 + (a_t - 1.0) * ka_ref[...])
    bq_t = kk * a_t
    ld_t = -jnp.exp(log_w)

    gw = group * RWKV_HEAD
    crow = lax.broadcasted_iota(jnp.int32, (chunk, group * chunk), 0)
    ccol = lax.broadcasted_iota(jnp.int32, (chunk, group * chunk), 1) & (chunk - 1)
    strict = ccol < crow
    incl = ccol <= crow
    diag = ccol == crow
    n_pow = int(math.log2(chunk))

    n_chunks = tc // chunk
    n_groups = RWKV_HEADS // group
    lanes = [slice(g * gw, (g + 1) * gw) for g in range(n_groups)]
    chains = [(c, g) for c in range(n_chunks) for g in range(n_groups)]

    a3, a3s, bb3, kb3, v3, kgbg3, v_c, g_end_c = {}, {}, {}, {}, {}, {}, {}, {}
    for c in range(n_chunks):
        sl = slice(c * chunk, (c + 1) * chunk)
        r, kpc, v, kkc, bq, ld = r_t[sl], kp[sl], v_t[sl], kk[sl], bq_t[sl], ld_t[sl]
        cum = ld
        d = 1
        while d < chunk:
            cum = cum + _shift_rows(cum, d, 0.0)
            d *= 2
        cum_end = cum[chunk - 1:chunk, :]
        inv = jnp.exp(-cum)
        tail_decay = jnp.exp(cum_end - cum)
        rt = r * jnp.exp(cum)
        kkt = kkc * jnp.exp(cum - ld)
        kb = kpc * inv
        bb = bq * inv
        kg = kpc * tail_decay
        bg = bq * tail_decay
        g_end_c[c] = jnp.exp(cum_end)
        for g, ls in enumerate(lanes):
            a_in = jnp.concatenate([kkt[:, ls], rt[:, ls]], axis=0)
            a3[c, g] = _lhs3(a_in, 1, PASSES_INTRA)
            a3s[c, g] = a3[c, g] if PASSES_STATE == PASSES_INTRA else _lhs3(a_in, 1, PASSES_STATE)
            bb3[c, g] = _rhs3(bb[:, ls], 1, group, PASSES_INTRA)
            kb3[c, g] = _rhs3(kb[:, ls], 1, group, PASSES_INTRA)
            v3[c, g] = _rhs3(v[:, ls], 0, group, PASSES_INTRA)
            kgbg3[c, g] = _rhs3(jnp.concatenate([kg[:, ls], -bg[:, ls]], axis=0), 0, 1, PASSES_STATE)
            v_c[c, g] = v[:, ls]

    xb = {ch: _dg(a3[ch], bb3[ch], 1, 1) for ch in chains}
    xk = {ch: _dg(a3[ch], kb3[ch], 1, 1) for ch in chains}
    n = {ch: jnp.where(strict, -xb[ch][0:chunk], 0.0) for ch in chains}
    rb3 = {ch: _lhs3(jnp.where(incl, xb[ch][chunk:], 0.0), 1, PASSES_OUT) for ch in chains}
    lv = {ch: _dg(_lhs3(jnp.concatenate([jnp.where(strict, xk[ch][0:chunk], 0.0),
                                          jnp.where(incl, xk[ch][chunk:], 0.0)], axis=0), 1, PASSES_INTRA),
                  v3[ch], 1, 0) for ch in chains}
    t = {ch: jnp.where(diag, 1.0, n[ch]) for ch in chains}
    p = {ch: _dg(_lhs3(n[ch], 1, PASSES_INV), _rhs3(n[ch], 0, group, PASSES_INV), 1, 0) for ch in chains}
    for q in range(1, n_pow):
        last = q + 1 == n_pow
        passes = PASSES_INV if q <= INV_ACCURATE_ROUNDS else 1
        tp = {ch: _dg(_lhs3(t[ch] if last else jnp.concatenate([t[ch], p[ch]], axis=0), 1, passes),
                      _rhs3(p[ch], 0, group, passes), 1, 0) for ch in chains}
        t = {ch: t[ch] + tp[ch][0:chunk] for ch in chains}
        if not last:
            p = {ch: tp[ch][chunk:] for ch in chains}
    t3 = {ch: _lhs3(t[ch], 1, PASSES_STATE) for ch in chains}

    y_rows = []
    for c in range(n_chunks):
        s_prev = [s_o[:, ls] for ls in lanes]
        xs = [_dg(a3s[c, g], _rhs3(s_prev[g], 1, group, PASSES_STATE), 1, 1) for g in range(n_groups)]
        u = [_dg(t3[c, g], _rhs3(xs[g][0:chunk] + lv[c, g][0:chunk], 0, group, PASSES_STATE), 1, 0)
             for g in range(n_groups)]
        ru = [_dg(rb3[c, g], _rhs3(u[g], 0, group, PASSES_OUT), 1, 0) for g in range(n_groups)]
        upd = [_dg(_lhs3(jnp.concatenate([v_c[c, g], u[g]], axis=0), 0, PASSES_STATE), kgbg3[c, g], 0, 0)
               for g in range(n_groups)]
        y_rows.append(jnp.concatenate(
            [xs[g][chunk:] + lv[c, g][chunk:] - ru[g] for g in range(n_groups)], axis=1))
        for g, ls in enumerate(lanes):
            s_o[:, ls] = s_prev[g] * g_end_c[c][:, ls] + _fold_diag(upd[g], group)
    y = jnp.concatenate(y_rows, axis=0)
    inv_n = 1.0 / RWKV_HEAD
    mean = head_sum(y) * inv_n
    yc = y - mean
    var = head_sum(yc * yc) * inv_n
    yn = yc * lax.rsqrt(var + RWKV_GN_EPS) * lnw_ref[...] + lnb_ref[...]
    bonus = head_sum(r_t * kp * rk_ref[...]) * v_t
    y_b = (yn + bonus) * g_t

    n_sub = tc // S5_SUB
    u_bf = u_s5.astype(BF16)
    lagged = jnp.dot(lag_shift_ref[...], u_bf, preferred_element_type=F32)
    lhs = jnp.concatenate(
        [u_bf] + [lagged[(d - 1) * tc:d * tc].astype(BF16) for d in range(1, S5_SUB)], axis=1)
    y_intra = jnp.dot(lhs, lagk_ref[...], preferred_element_type=F32)

    bu_re = _mm(u_bf, bbar_re[...])
    bu_im = _mm(u_bf, bbar_im[...])
    te_re = jnp.concatenate([tab_end_re[...]] * n_sub, axis=0)
    te_im = jnp.concatenate([tab_end_im[...]] * n_sub, axis=0)
    sub_sum = sub_sum_ref[...]
    e_re = _mm(sub_sum, (te_re * bu_re - te_im * bu_im).astype(BF16))[0:n_sub]
    e_im = _mm(sub_sum, (te_re * bu_im + te_im * bu_re).astype(BF16))[0:n_sub]
    h0r = s5re_o[...]
    h0i = s5im_o[...]
    k_sub = int(math.log2(S5_SUB))
    ar = pw_re[k_sub:k_sub + 1, :]
    ai = pw_im[k_sub:k_sub + 1, :]
    rn = lax.broadcasted_iota(jnp.int32, (n_sub, 1), 0)
    hr = e_re + jnp.where(rn == 0, ar * h0r - ai * h0i, 0.0)
    hi = e_im + jnp.where(rn == 0, ar * h0i + ai * h0r, 0.0)
    d = 1
    while d < n_sub:
        k = k_sub + int(math.log2(d))
        pr = pw_re[k:k + 1, :]
        pi = pw_im[k:k + 1, :]
        sr = _shift_rows(hr, d, 0.0)
        si = _shift_rows(hi, d, 0.0)
        hr, hi = hr + (pr * sr - pi * si), hi + (pr * si + pi * sr)
        d *= 2
    s5re_o[...] = hr[n_sub - 1:n_sub, :]
    s5im_o[...] = hi[n_sub - 1:n_sub, :]
    if n_sub > 1:
        pvr = jnp.where(rn == 0, h0r, pltpu.roll(hr, 1, axis=0))
        pvi = jnp.where(rn == 0, h0i, pltpu.roll(hi, 1, axis=0))
    else:
        pvr, pvi = h0r, h0i
    pvr = jnp.broadcast_to(pvr[:, None, :], (n_sub, S5_SUB, S5_STATES)).reshape(tc, S5_STATES)
    pvi = jnp.broadcast_to(pvi[:, None, :], (n_sub, S5_SUB, S5_STATES)).reshape(tc, S5_STATES)
    ti_re = jnp.concatenate([tab_in_re[...]] * n_sub, axis=0)
    ti_im = jnp.concatenate([tab_in_im[...]] * n_sub, axis=0)
    y_inter = (_mm(ti_re * pvr - ti_im * pvi, cre_ref[...])
               - _mm(ti_re * pvi + ti_im * pvr, cim_ref[...]))
    y5 = y_intra + y_inter + d_ref[...] * u_s5
    z = jax.nn.gelu(y5)
    y_c = z * jax.nn.sigmoid(_mm(z, gluw_ref[...]) + glub_ref[...])

    mix = _mm(jnp.concatenate([y_a, y_b, y_c], axis=-1), w_out_ref[...])
    xo_ref[...] = x + mod_ref[2:3, :] * mix


def _lag_shift_matrix(tc):
    t = jnp.arange(tc)
    blocks = [((t[:, None] - d == t[None, :]) & ((t % S5_SUB) >= d)[:, None]) for d in range(1, S5_SUB)]
    return jnp.concatenate(blocks, axis=0).astype(BF16)


def _const_spec(arr):
    nd = arr.ndim
    return pl.BlockSpec(arr.shape, lambda b, i: (0,) * nd)


def _batch_spec(arr):
    nd = arr.ndim
    return pl.BlockSpec((None,) + arr.shape[1:], lambda b, i: (b,) + (0,) * (nd - 1))


def _mixer(tl, x, mod, lp, st):
    bsz, seq, _ = x.shape
    tc = tl.t_mix
    consts1 = [lp["norm_mix"], lp["w_in"], lp["w_out"],
               lp["lru_conv_w"], lp["lru_conv_b"], lp["lru_wa"], lp["lru_ba"], lp["lru_wx"],
               lp["lru_bx"], lp["lru_lambda"]]
    states1 = [st["lru_buf"], st["lru_h"]]
    consts2 = [lp["rwkv_mu"], lp["rwkv_w0"], lp["rwkv_w2"], lp["rwkv_a0"], lp["rwkv_a2"],
               lp["rwkv_g2"], lp["rwkv_k_k"], lp["rwkv_k_a"], lp["rwkv_r_k"], lp["rwkv_ln_w"],
               lp["rwkv_ln_b"], lp["ones_bd"]]
    states2 = [st["rwkv_shift"], st["rwkv_s"]]
    sub_rows = max(tc // S5_SUB, SUBLANES)
    sub_sum = (jnp.arange(tc)[None, :] // S5_SUB == jnp.arange(sub_rows)[:, None]).astype(BF16)
    consts3 = list(lp["s5_tables"]) + [_lag_shift_matrix(tc), sub_sum, lp["s5_c_re"], lp["s5_c_im"],
                                       lp["s5_d"], lp["s5_glu_w"], lp["s5_glu_b"]]
    states3 = [st["s5_re"], st["s5_im"]]
    x_spec = pl.BlockSpec((None, tc, D_MODEL), lambda b, i: (b, i, 0))
    in_specs = ([x_spec, _batch_spec(mod)]
                + [_const_spec(a) for a in consts1] + [_batch_spec(a) for a in states1]
                + [_const_spec(a) for a in consts2] + [_batch_spec(a) for a in states2]
                + [_const_spec(a) for a in consts3] + [_batch_spec(a) for a in states3])
    state_list = states1 + states2 + states3
    out_shape = [jax.ShapeDtypeStruct(x.shape, F32)] + [jax.ShapeDtypeStruct(a.shape, F32) for a in state_list]
    out_specs = [x_spec] + [_batch_spec(a) for a in state_list]
    outs = pl.pallas_call(
        functools.partial(_mixer_kernel, tl),
        grid=(bsz, seq // tc),
        in_specs=in_specs, out_specs=out_specs, out_shape=out_shape,
        compiler_params=pltpu.CompilerParams(
            dimension_semantics=("arbitrary", "arbitrary"), vmem_limit_bytes=VMEM_LIMIT_BYTES),
        name="mixer",
    )(x, mod, *consts1, *states1, *consts2, *states2, *consts3, *states3)
    x_new = outs[0]
    keys = ["lru_buf", "lru_h", "rwkv_shift", "rwkv_s", "s5_re", "s5_im"]
    return x_new, dict(zip(keys, outs[1:]))


def _ffn_kernel(tl, final,
                x_ref, mod_ref, norm_ref, up_ref, cw_ref, cb_ref, down_ref, tail0_ref, fin_ref,
                xo_ref, tail_o):
    tf = tl.t_ffn
    i = pl.program_id(1)

    @pl.when(i == 0)
    def _init():
        tail_o[...] = tail0_ref[...]

    x = x_ref[...]
    h = _rmsnorm_mod(x, norm_ref[...], mod_ref[4:5, :], mod_ref[3:4, :]).astype(BF16)

    def up_block(c0):
        return jnp.dot(h, up_ref[:, c0:c0 + FFN_COL_BLOCK], preferred_element_type=F32)

    def conv_block(up, c0):
        cs = slice(c0, c0 + FFN_COL_BLOCK)
        tail = tail_o[:, cs]
        m1 = _shift_in(up, 1, tail)
        m2 = _shift_in(up, 2, tail)
        tail_o[:, cs] = up[tf - SUBLANES:tf, :]
        return cb_ref[:, cs] + m2 * cw_ref[0:1, cs] + m1 * cw_ref[1:2, cs] + up * cw_ref[2:3, cs]

    blocks = list(range(0, D_FF, FFN_COL_BLOCK))
    pending = (up_block(blocks[0]), up_block(D_FF + blocks[0]))
    acc = None
    acts = []
    for j, c0 in enumerate(blocks):
        up_val, up_gate = pending
        if j + 1 < len(blocks):
            pending = (up_block(blocks[j + 1]), up_block(D_FF + blocks[j + 1]))
        val = conv_block(up_val, c0)
        gate = conv_block(up_gate, D_FF + c0)
        acts.append((val * (gate * jax.nn.sigmoid(gate))).astype(BF16))
        if len(acts) == FFN_DOWN_GROUP or j + 1 == len(blocks):
            r0 = c0 + FFN_COL_BLOCK - len(acts) * FFN_COL_BLOCK
            part = jnp.dot(jnp.concatenate(acts, axis=1), down_ref[r0:c0 + FFN_COL_BLOCK, :],
                           preferred_element_type=F32)
            acc = part if acc is None else acc + part
            acts = []
    y = x + mod_ref[5:6, :] * acc
    if final:
        y = y * lax.rsqrt(jnp.mean(y * y, axis=-1, keepdims=True) + NORM_EPS) * fin_ref[...]
    xo_ref[...] = y


def _ffn(tl, final, x, mod, lp, tail0, norm_final):
    bsz, seq, _ = x.shape
    tf = tl.t_ffn
    consts = [lp["norm_ffn"], lp["ffn_up"], lp["ffn_conv_w"], lp["ffn_conv_b"], lp["ffn_down"]]
    x_spec = pl.BlockSpec((None, tf, D_MODEL), lambda b, i: (b, i, 0))
    in_specs = ([x_spec, _batch_spec(mod)] + [_const_spec(a) for a in consts]
                + [_batch_spec(tail0), _const_spec(norm_final)])
    x_new, tail = pl.pallas_call(
        functools.partial(_ffn_kernel, tl, final),
        grid=(bsz, seq // tf),
        in_specs=in_specs,
        out_specs=[x_spec, _batch_spec(tail0)],
        out_shape=[jax.ShapeDtypeStruct(x.shape, F32), jax.ShapeDtypeStruct(tail0.shape, F32)],
        compiler_params=pltpu.CompilerParams(
            dimension_semantics=("arbitrary", "arbitrary"), vmem_limit_bytes=VMEM_LIMIT_BYTES),
        name="conv_ffn",
    )(x, mod, *consts, tail0, norm_final)
    return x_new, tail


def _block_diag_dense(blocks):
    n, r, c = blocks.shape
    eye = jnp.eye(n, dtype=blocks.dtype)
    return (eye[:, None, :, None] * blocks[:, :, None, :]).reshape(n * r, n * c)


def _layer_params(p, l):
    row = lambda a: a.reshape(1, -1)
    zeros_lora = jnp.zeros((64, RWKV_WIDTH), F32)
    head = jnp.arange(RWKV_WIDTH) // RWKV_HEAD
    lp = {
        "norm_mix": row(p["norm_mix"][l]), "norm_ffn": row(p["norm_ffn"][l]),
        "w_in": p["w_in"][l].astype(BF16), "w_out": p["w_out"][l].astype(BF16),
        "lru_conv_w": p["lru_conv_w"][l], "lru_conv_b": row(p["lru_conv_b"][l]),
        "lru_wa": _block_diag_dense(p["lru_wa"][l]).astype(BF16), "lru_ba": row(p["lru_ba"][l]),
        "lru_wx": _block_diag_dense(p["lru_wx"][l]).astype(BF16), "lru_bx": row(p["lru_bx"][l]),
        "lru_lambda": row(p["lru_lambda"][l]),
        "rwkv_mu": row(p["rwkv_mu"][l]), "rwkv_w0": row(p["rwkv_w0"][l]),
        "rwkv_w2": jnp.concatenate([p["rwkv_w2"][l], zeros_lora], axis=0).astype(BF16),
        "rwkv_a0": row(p["rwkv_a0"][l]),
        "rwkv_a2": jnp.concatenate([zeros_lora, p["rwkv_a2"][l]], axis=0).astype(BF16),
        "rwkv_g2": p["rwkv_g2"][l].astype(BF16),
        "rwkv_k_k": row(p["rwkv_k_k"][l]), "rwkv_k_a": row(p["rwkv_k_a"][l]),
        "rwkv_r_k": row(p["rwkv_r_k"][l]), "rwkv_ln_w": row(p["rwkv_ln_w"][l]),
        "rwkv_ln_b": row(p["rwkv_ln_b"][l]),
        "ones_bd": jnp.tile(head[:, None] == head[None, :], (2, 1)).astype(BF16),
        "s5_a_re": row(p["s5_a_re"][l]), "s5_a_im": row(p["s5_a_im"][l]),
        "s5_log_dt": row(jnp.repeat(p["s5_log_dt"][l], 64)),
        "s5_b_re": _block_diag_dense(jnp.swapaxes(p["s5_b_re"][l], 1, 2)),
        "s5_b_im": _block_diag_dense(jnp.swapaxes(p["s5_b_im"][l], 1, 2)),
        "s5_c_re_f32": _block_diag_dense(jnp.swapaxes(p["s5_c_re"][l], 1, 2)),
        "s5_c_im_f32": _block_diag_dense(jnp.swapaxes(p["s5_c_im"][l], 1, 2)),
        "s5_d": row(p["s5_d"][l]),
        "s5_glu_w": p["s5_glu_w"][l].astype(BF16), "s5_glu_b": row(p["s5_glu_b"][l]),
        "ffn_up": p["ffn_up"][l].astype(BF16), "ffn_conv_w": p["ffn_conv_w"][l],
        "ffn_conv_b": row(p["ffn_conv_b"][l]), "ffn_down": p["ffn_down"][l].astype(BF16),
    }
    lp["s5_tables"] = _s5_prep(lp)
    lp["s5_c_re"] = lp["s5_c_re_f32"].astype(BF16)
    lp["s5_c_im"] = lp["s5_c_im_f32"].astype(BF16)
    return lp


def _pad_tail(buf):
    return jnp.pad(buf, ((0, 0), (SUBLANES - buf.shape[1], 0), (0, 0)))


def _state_in(states, l):
    lru_buf, lru_h, rw_shift, rw_s, s5_re, s5_im, ffn_buf = (s[l] for s in states)
    bsz = lru_h.shape[0]
    return {
        "lru_buf": _pad_tail(lru_buf),
        "lru_h": lru_h.reshape(bsz, 1, LRU_WIDTH),
        "rwkv_shift": rw_shift.reshape(bsz, 1, RWKV_COLS),
        "rwkv_s": jnp.transpose(rw_s, (0, 2, 1, 3)).reshape(bsz, RWKV_HEAD, RWKV_WIDTH),
        "s5_re": s5_re.reshape(bsz, 1, S5_STATES),
        "s5_im": s5_im.reshape(bsz, 1, S5_STATES),
    }, _pad_tail(ffn_buf)


def _state_out(st, ffn_tail):
    bsz = st["lru_h"].shape[0]
    return (st["lru_buf"][:, SUBLANES - (LRU_CONV - 1):, :],
            st["lru_h"].reshape(bsz, LRU_WIDTH),
            st["rwkv_shift"].reshape(bsz, RWKV_COLS),
            jnp.transpose(st["rwkv_s"].reshape(bsz, RWKV_HEAD, RWKV_HEADS, RWKV_HEAD), (0, 2, 1, 3)),
            st["s5_re"].reshape(bsz, 16, 64),
            st["s5_im"].reshape(bsz, 16, 64),
            ffn_tail[:, SUBLANES - 2:, :])


def _trunk(x, mods, states, layer_params, norm_final):
    depth = len(layer_params)
    tl = _tiling(x.shape[1])
    new = []
    for l in range(depth):
        st, ffn_tail0 = _state_in(states, l)
        x, st = _mixer(tl, x, mods[l], layer_params[l], st)
        x, ffn_tail = _ffn(tl, l == depth - 1, x, mods[l], layer_params[l], ffn_tail0, norm_final)
        new.append(_state_out(st, ffn_tail))
    stacked = tuple(jnp.stack([n[j] for n in new], axis=0) for j in range(7))
    return x, stacked


def _zero_states(depth, bsz):
    return (jnp.zeros((depth, bsz, LRU_CONV - 1, LRU_WIDTH), F32),
            jnp.zeros((depth, bsz, LRU_WIDTH), F32),
            jnp.zeros((depth, bsz, RWKV_COLS), F32),
            jnp.zeros((depth, bsz, RWKV_HEADS, RWKV_HEAD, RWKV_HEAD), F32),
            jnp.zeros((depth, bsz, 16, 64), F32),
            jnp.zeros((depth, bsz, 16, 64), F32),
            jnp.zeros((depth, bsz, 2, 2 * D_FF), F32))


def kernel(x_prompt, x_sample, c_prompt, c_sample, state_lru_conv, state_lru_h, state_rwkv_shift, state_rwkv_S, state_s5_re, state_s5_im, state_ffn_conv, w_ada, b_ada, norm_mix, norm_ffn, w_in, w_out, lru_conv_w, lru_conv_b, lru_wa, lru_ba, lru_wx, lru_bx, lru_lambda, rwkv_mu, rwkv_w0, rwkv_w2, rwkv_a0, rwkv_a2, rwkv_g2, rwkv_k_k, rwkv_k_a, rwkv_r_k, rwkv_ln_w, rwkv_ln_b, s5_a_re, s5_a_im, s5_b_re, s5_b_im, s5_c_re, s5_c_im, s5_d, s5_log_dt, s5_glu_w, s5_glu_b, ffn_up, ffn_conv_w, ffn_conv_b, ffn_down, norm_final):
    p = dict(norm_mix=norm_mix, norm_ffn=norm_ffn, w_in=w_in, w_out=w_out,
             lru_conv_w=lru_conv_w, lru_conv_b=lru_conv_b, lru_wa=lru_wa, lru_ba=lru_ba,
             lru_wx=lru_wx, lru_bx=lru_bx, lru_lambda=lru_lambda,
             rwkv_mu=rwkv_mu, rwkv_w0=rwkv_w0, rwkv_w2=rwkv_w2, rwkv_a0=rwkv_a0, rwkv_a2=rwkv_a2,
             rwkv_g2=rwkv_g2, rwkv_k_k=rwkv_k_k, rwkv_k_a=rwkv_k_a,
             rwkv_r_k=rwkv_r_k.reshape(rwkv_r_k.shape[0], RWKV_WIDTH),
             rwkv_ln_w=rwkv_ln_w, rwkv_ln_b=rwkv_ln_b,
             s5_a_re=s5_a_re, s5_a_im=s5_a_im, s5_b_re=s5_b_re, s5_b_im=s5_b_im,
             s5_c_re=s5_c_re, s5_c_im=s5_c_im, s5_d=s5_d, s5_log_dt=s5_log_dt,
             s5_glu_w=s5_glu_w, s5_glu_b=s5_glu_b,
             ffn_up=ffn_up, ffn_conv_w=ffn_conv_w, ffn_conv_b=ffn_conv_b, ffn_down=ffn_down)
    depth = w_in.shape[0]
    layer_params = [_layer_params(p, l) for l in range(depth)]
    fin = norm_final.reshape(1, D_MODEL)
    n_prompt = c_prompt.shape[0]
    mod = _ada(jnp.concatenate([c_prompt, c_sample], axis=0), w_ada, b_ada)
    mod = mod.reshape(depth, mod.shape[1], 6, D_MODEL)
    mods_p = [mod[l, :n_prompt] for l in range(depth)]
    mods_s = [mod[l, n_prompt:] for l in range(depth)]
    s_in = (state_lru_conv, state_lru_h, state_rwkv_shift, state_rwkv_S, state_s5_re, state_s5_im,
            state_ffn_conv)
    y_sample, s_states = _trunk(x_sample, mods_s, s_in, layer_params, fin)
    y_prompt, p_states = _trunk(x_prompt, mods_p, _zero_states(depth, x_prompt.shape[0]), layer_params, fin)
    return (y_prompt, y_sample) + tuple(p_states) + tuple(s_states)
```

```python
import functools
import math
from typing import NamedTuple

import jax
import jax.numpy as jnp
from jax import lax
from jax.experimental import pallas as pl
from jax.experimental.pallas import tpu as pltpu

F32 = jnp.float32
BF16 = jnp.bfloat16

D_MODEL = 1024
LRU_WIDTH = 256
LRU_CONV = 4
LRU_C = 8.0
RWKV_WIDTH = 512
RWKV_HEAD = 64
RWKV_HEADS = 8
RWKV_COLS = 1792
S5_WIDTH = 256
S5_STATES = 1024
S5_POW_ROWS = 8
S5_SUB = 8
IN_COLS = 2560
D_FF = 2816
NORM_EPS = 1e-6
RWKV_GN_EPS = 64e-5
SUBLANES = 8
FFN_COL_BLOCK = 256
FFN_DOWN_GROUP = 4
VMEM_LIMIT_BYTES = 56 * 1024 * 1024
PASSES_INTRA = 1
PASSES_INV = 3
INV_ACCURATE_ROUNDS = 3
PASSES_STATE = 1
PASSES_OUT = 1


class Tiling(NamedTuple):
    t_mix: int
    chunk: int
    group: int
    t_ffn: int


def _tiling(seq_len):
    t_mix = min(seq_len, 256)
    chunk = min(t_mix, 64)
    group = min(RWKV_HEADS, 256 // chunk)
    return Tiling(t_mix, chunk, group, min(seq_len, 512))


def _dg(a, b, ca, cb, precision=None):
    return lax.dot_general(a, b, (((ca,), (cb,)), ((), ())), precision=precision,
                           preferred_element_type=F32)


def _mm(a, w):
    return jnp.dot(a.astype(BF16), w, preferred_element_type=F32)


def _expm1(x):
    u = jnp.exp(x)
    small = u >= 0.5
    stable = (u - 1.0) * x / jnp.log(jnp.where(small, u, 0.75))
    return jnp.where(u == 1.0, x, jnp.where(small, stable, u - 1.0))


def _shift_rows(x, d, fill):
    if d % SUBLANES == 0:
        return jnp.concatenate([jnp.full((d, x.shape[1]), fill, x.dtype), x[:x.shape[0] - d]], axis=0)
    rolled = pltpu.roll(x, d, axis=0)
    if x.shape[0] <= SUBLANES:
        return jnp.where(lax.broadcasted_iota(jnp.int32, x.shape, 0) >= d, rolled, fill)
    r8 = lax.broadcasted_iota(jnp.int32, (SUBLANES, 1), 0)
    return jnp.concatenate([jnp.where(r8 >= d, rolled[0:SUBLANES], fill), rolled[SUBLANES:]], axis=0)


def _shift_in(x, d, tail):
    rolled = pltpu.roll(x, d, axis=0)
    head = rolled[0:SUBLANES]
    r8 = lax.broadcasted_iota(jnp.int32, (SUBLANES, 1), 0)
    for j in range(d):
        head = jnp.where(r8 == j, tail[SUBLANES - d + j:SUBLANES - d + j + 1], head)
    return jnp.concatenate([head, rolled[SUBLANES:]], axis=0)


def _linear_scan(a, b):
    d = 1
    while d < a.shape[0]:
        b = a * _shift_rows(b, d, 0.0) + b
        a = a * _shift_rows(a, d, 1.0)
        d *= 2
    return b


def _block_mask(rows, cols, rb, cb, dtype):
    r = lax.broadcasted_iota(jnp.int32, (rows, cols), 0) >> int(math.log2(rb))
    c = lax.broadcasted_iota(jnp.int32, (rows, cols), 1) >> int(math.log2(cb))
    return jnp.where(r == c, 1.0, 0.0).astype(dtype)


def _split(x):
    hi = x.astype(BF16)
    return hi, (x - hi.astype(F32)).astype(BF16)


def _lhs3(a, axis, passes=3):
    if passes == 1:
        return a.astype(BF16)
    hi, lo = _split(a)
    return jnp.concatenate([hi, hi, lo], axis=axis)


def _rhs3(b, axis, group=1, passes=3):
    parts = (b.astype(BF16),) if passes == 1 else _split(b)
    if group > 1:
        rows, cols = b.shape
        mask = _block_mask(group * rows, cols, rows, cols // group, BF16)
        parts = tuple(jnp.concatenate([q] * group, axis=0) * mask for q in parts)
    if passes == 1:
        return parts[0]
    return jnp.concatenate([parts[0], parts[1], parts[0]], axis=axis)


def _fold_diag(m, group):
    rows = m.shape[0] // group
    width = m.shape[1] // group
    rb = lax.broadcasted_iota(jnp.int32, m.shape, 0) >> int(math.log2(rows))
    cb = lax.broadcasted_iota(jnp.int32, m.shape, 1) >> int(math.log2(width))
    m = jnp.where(rb == cb, m, 0.0)
    out = m[0:rows]
    for h in range(1, group):
        out = out + m[h * rows:(h + 1) * rows]
    return out


def _rmsnorm_mod(x, gain, scale, shift):
    y = x * lax.rsqrt(jnp.mean(x * x, axis=-1, keepdims=True) + NORM_EPS)
    return (y * gain) * (1.0 + scale) + shift


def _ada_kernel(c_ref, w_ref, b_ref, o_ref):
    c = c_ref[...]
    o_ref[...] = _mm(c * jax.nn.sigmoid(c), w_ref[...].astype(BF16)) + b_ref[...]


def _ada(c, w_ada, b_ada):
    depth, _, n = w_ada.shape
    rows = c.shape[0]
    tn = 1536
    return pl.pallas_call(
        _ada_kernel,
        grid=(depth, n // tn),
        in_specs=[pl.BlockSpec((rows, D_MODEL), lambda l, j: (0, 0)),
                  pl.BlockSpec((None, D_MODEL, tn), lambda l, j: (l, 0, j)),
                  pl.BlockSpec((None, 1, tn), lambda l, j: (l, 0, j))],
        out_specs=pl.BlockSpec((None, rows, tn), lambda l, j: (l, 0, j)),
        out_shape=jax.ShapeDtypeStruct((depth, rows, n), F32),
        compiler_params=pltpu.CompilerParams(
            dimension_semantics=("arbitrary", "arbitrary"), vmem_limit_bytes=VMEM_LIMIT_BYTES),
        name="ada_mod",
    )(c, w_ada, b_ada.reshape(depth, 1, n))


def _s5_prep_kernel(are_ref, aim_ref, ldt_ref, bre_ref, bim_ref, cre_ref, cim_ref,
                    bbar_re, bbar_im, pw_re, pw_im, lagk, tab_end_re, tab_end_im, tab_in_re, tab_in_im):
    dt = jnp.exp(ldt_ref[...])
    a_re = are_ref[...]
    a_im = aim_ref[...]
    mag = jnp.exp(dt * a_re)
    abr = mag * jnp.cos(dt * a_im)
    abi = mag * jnp.sin(dt * a_im)
    den = a_re * a_re + a_im * a_im
    fr = ((abr - 1.0) * a_re + abi * a_im) / den
    fi = (abi * a_re - (abr - 1.0) * a_im) / den
    bb_re = fr * bre_ref[...] - fi * bim_ref[...]
    bb_im = fr * bim_ref[...] + fi * bre_ref[...]
    bbar_re[...] = bb_re.astype(BF16)
    bbar_im[...] = bb_im.astype(BF16)
    pr, pi = abr, abi
    for k in range(S5_POW_ROWS):
        pw_re[k:k + 1, :] = pr
        pw_im[k:k + 1, :] = pi
        pr, pi = pr * pr - pi * pi, 2.0 * pr * pi
    c3_re = _rhs3(cre_ref[...], 0)
    c3_im = _rhs3(cim_ref[...], 0)
    pr, pi = jnp.ones_like(abr), jnp.zeros_like(abi)
    for d in range(S5_SUB + 1):
        if d < S5_SUB:
            lagk[d * S5_WIDTH:(d + 1) * S5_WIDTH, :] = (
                _dg(_lhs3(bb_re * pr - bb_im * pi, 1), c3_re, 1, 0)
                - _dg(_lhs3(bb_re * pi + bb_im * pr, 1), c3_im, 1, 0)).astype(BF16)
            tab_end_re[S5_SUB - 1 - d:S5_SUB - d, :] = pr
            tab_end_im[S5_SUB - 1 - d:S5_SUB - d, :] = pi
        if d > 0:
            tab_in_re[d - 1:d, :] = pr
            tab_in_im[d - 1:d, :] = pi
        pr, pi = pr * abr - pi * abi, pr * abi + pi * abr


def _s5_prep(lp):
    ins = [lp["s5_a_re"], lp["s5_a_im"], lp["s5_log_dt"], lp["s5_b_re"], lp["s5_b_im"],
           lp["s5_c_re_f32"], lp["s5_c_im_f32"]]
    return pl.pallas_call(
        _s5_prep_kernel,
        out_shape=[jax.ShapeDtypeStruct((S5_WIDTH, S5_STATES), BF16)] * 2
        + [jax.ShapeDtypeStruct((S5_POW_ROWS, S5_STATES), F32)] * 2
        + [jax.ShapeDtypeStruct((S5_SUB * S5_WIDTH, S5_WIDTH), BF16)]
        + [jax.ShapeDtypeStruct((S5_SUB, S5_STATES), F32)] * 4,
        compiler_params=pltpu.CompilerParams(vmem_limit_bytes=VMEM_LIMIT_BYTES),
        name="s5_prep",
    )(*ins)


def _mixer_kernel(tl,
                  x_ref, mod_ref, norm_ref, w_in_ref, w_out_ref,
                  conv_w_ref, conv_b_ref, wa_ref, ba_ref, wx_ref, bx_ref, lam_ref,
                  lru_buf0_ref, lru_h0_ref,
                  mu_ref, w0_ref, w2_ref, a0_ref, a2_ref, g2_ref, kk_ref, ka_ref, rk_ref,
                  lnw_ref, lnb_ref, ones_ref, shift0_ref, s0_ref,
                  bbar_re, bbar_im, pw_re, pw_im, lagk_ref, tab_end_re, tab_end_im, tab_in_re, tab_in_im,
                  lag_shift_ref, sub_sum_ref, cre_ref, cim_ref, d_ref,
                  gluw_ref, glub_ref, s5re0_ref, s5im0_ref,
                  xo_ref, lru_buf_o, lru_h_o, shift_o, s_o, s5re_o, s5im_o):
    tc, chunk, group = tl.t_mix, tl.chunk, tl.group
    n_scan = int(math.log2(tc))
    assert n_scan <= S5_POW_ROWS
    i = pl.program_id(1)

    @pl.when(i == 0)
    def _init():
        lru_buf_o[...] = lru_buf0_ref[...]
        lru_h_o[...] = lru_h0_ref[...]
        shift_o[...] = shift0_ref[...]
        s_o[...] = s0_ref[...]
        s5re_o[...] = s5re0_ref[...]
        s5im_o[...] = s5im0_ref[...]

    x = x_ref[...]
    r8 = lax.broadcasted_iota(jnp.int32, (SUBLANES, 1), 0)
    h = _rmsnorm_mod(x, norm_ref[...], mod_ref[1:2, :], mod_ref[0:1, :])
    proj = _mm(h, w_in_ref[...])
    u_gate = proj[:, 0:256]
    u_lru = proj[:, 256:512]
    p_rw = proj[:, 512:2304]
    u_s5 = proj[:, 2304:2560]

    branch = {}

    def lru_step():
        cw = conv_w_ref[...]
        lru_tail = lru_buf_o[...]
        xc = conv_b_ref[...] + _shift_in(u_lru, LRU_CONV - 1, lru_tail) * cw[0:1]
        for k in range(1, LRU_CONV - 1):
            xc = xc + _shift_in(u_lru, LRU_CONV - 1 - k, lru_tail) * cw[k:k + 1]
        xc = xc + u_lru * cw[LRU_CONV - 1:LRU_CONV]
        lru_buf_o[...] = u_lru[tc - SUBLANES:tc, :]
        r_gate = jax.nn.sigmoid(_mm(xc, wa_ref[...]) + ba_ref[...])
        i_gate = jax.nn.sigmoid(_mm(xc, wx_ref[...]) + bx_ref[...])
        log_a = -LRU_C * r_gate * jax.nn.softplus(-lam_ref[...])
        a = jnp.exp(log_a)
        gain = jnp.sqrt(jnp.maximum(-_expm1(2.0 * log_a), 0.0))
        b = gain * i_gate * xc
        b = jnp.concatenate([b[0:SUBLANES] + jnp.where(r8 == 0, a[0:SUBLANES] * lru_h_o[...], 0.0),
                             b[SUBLANES:]], axis=0)
        h_lru = _linear_scan(a, b)
        lru_h_o[...] = h_lru[tc - 1:tc, :]
        branch["y_a"] = h_lru * jax.nn.gelu(u_gate)

    n_sub = tc // S5_SUB
    u_bf = u_s5.astype(BF16)

    def s5_lag_step():
        lagged = jnp.dot(lag_shift_ref[...], u_bf, preferred_element_type=F32)
        branch["lags"] = jnp.concatenate(
            [u_bf] + [lagged[(d - 1) * tc:d * tc].astype(BF16) for d in range(1, S5_SUB)], axis=1)

    def s5_intra_step():
        branch["y_intra"] = jnp.dot(branch["lags"], lagk_ref[...], preferred_element_type=F32)

    def s5_state_step():
        bu_re = _mm(u_bf, bbar_re[...])
        bu_im = _mm(u_bf, bbar_im[...])
        te_re = jnp.concatenate([tab_end_re[...]] * n_sub, axis=0)
        te_im = jnp.concatenate([tab_end_im[...]] * n_sub, axis=0)
        sub_sum = sub_sum_ref[...]
        e_re = _mm(sub_sum, (te_re * bu_re - te_im * bu_im).astype(BF16))[0:n_sub]
        e_im = _mm(sub_sum, (te_re * bu_im + te_im * bu_re).astype(BF16))[0:n_sub]
        h0r = s5re_o[...]
        h0i = s5im_o[...]
        k_sub = int(math.log2(S5_SUB))
        ar = pw_re[k_sub:k_sub + 1, :]
        ai = pw_im[k_sub:k_sub + 1, :]
        if n_sub <= SUBLANES:
            cr, ci = h0r, h0i
            prev_r, prev_i = [], []
            for n in range(n_sub):
                prev_r.append(jnp.broadcast_to(cr, (S5_SUB, S5_STATES)))
                prev_i.append(jnp.broadcast_to(ci, (S5_SUB, S5_STATES)))
                cr, ci = ar * cr - ai * ci + e_re[n:n + 1], ar * ci + ai * cr + e_im[n:n + 1]
            s5re_o[...] = cr
            s5im_o[...] = ci
            pvr = jnp.concatenate(prev_r, axis=0)
            pvi = jnp.concatenate(prev_i, axis=0)
        else:
            rn = lax.broadcasted_iota(jnp.int32, (n_sub, 1), 0)
            hr = e_re + jnp.where(rn == 0, ar * h0r - ai * h0i, 0.0)
            hi = e_im + jnp.where(rn == 0, ar * h0i + ai * h0r, 0.0)
            d = 1
            while d < n_sub:
                k = k_sub + int(math.log2(d))
                pr = pw_re[k:k + 1, :]
                pi = pw_im[k:k + 1, :]
                sr = _shift_rows(hr, d, 0.0)
                si = _shift_rows(hi, d, 0.0)
                hr, hi = hr + (pr * sr - pi * si), hi + (pr * si + pi * sr)
                d *= 2
            s5re_o[...] = hr[n_sub - 1:n_sub, :]
            s5im_o[...] = hi[n_sub - 1:n_sub, :]
            pvr = jnp.where(rn == 0, h0r, pltpu.roll(hr, 1, axis=0))
            pvi = jnp.where(rn == 0, h0i, pltpu.roll(hi, 1, axis=0))
            pvr = jnp.broadcast_to(pvr[:, None, :], (n_sub, S5_SUB, S5_STATES)).reshape(tc, S5_STATES)
            pvi = jnp.broadcast_to(pvi[:, None, :], (n_sub, S5_SUB, S5_STATES)).reshape(tc, S5_STATES)
        branch["pv"] = (pvr, pvi)

    def s5_out_step():
        pvr, pvi = branch["pv"]
        ti_re = jnp.concatenate([tab_in_re[...]] * n_sub, axis=0)
        ti_im = jnp.concatenate([tab_in_im[...]] * n_sub, axis=0)
        y_inter = (_mm(ti_re * pvr - ti_im * pvi, cre_ref[...])
                   - _mm(ti_re * pvi + ti_im * pvr, cim_ref[...]))
        y5 = branch["y_intra"] + y_inter + d_ref[...] * u_s5
        z = jax.nn.gelu(y5)
        branch["y_c"] = z * jax.nn.sigmoid(_mm(z, gluw_ref[...]) + glub_ref[...])

    fillers = iter([s5_lag_step, lru_step, s5_intra_step, s5_state_step, s5_out_step])

    def fill():
        step = next(fillers, None)
        if step is not None:
            step()

    rolled = pltpu.roll(p_rw, 1, axis=0)
    prev =jnp.concatenate([jnp.where(r8 == 0, shift_o[...], rolled[0:SUBLANES]), rolled[SUBLANES:]], axis=0)
    shift_o[...] = p_rw[tc - 1:tc, :]
    xm = p_rw + (prev - p_rw) * mu_ref[...]
    r_t = xm[:, 0:512]
    k_t = xm[:, 512:1024]
    v_t = xm[:, 1024:1536]
    lo = xm[:, 1536:1664]
    g_lo = xm[:, 1664:1792]
    ones_bd = ones_ref[...]

    def head_sum(z):
        w = ones_bd.shape[0]
        return jnp.concatenate([_mm(z[:, j:j + w], ones_bd) for j in range(0, RWKV_WIDTH, w)], axis=1)

    log_w = -jax.nn.softplus(-(w0_ref[...] + _mm(jnp.tanh(lo), w2_ref[...]))) - 0.5
    a_t = jax.nn.sigmoid(a0_ref[...] + _mm(lo, a2_ref[...]))
    g_t = _mm(jax.nn.sigmoid(g_lo), g2_ref[...])
    kk = k_t * kk_ref[...]
    kk = kk * lax.rsqrt(jnp.maximum(head_sum(kk * kk), 1e-24))
    kp = k_t * (1.0 + (a_t - 1.0) * ka_ref[...])
    bq_t = kk * a_t
    ld_t = -jnp.exp(log_w)

    gw = group * RWKV_HEAD
    crow = lax.broadcasted_iota(jnp.int32, (chunk, group * chunk), 0)
    ccol = lax.broadcasted_iota(jnp.int32, (chunk, group * chunk), 1) & (chunk - 1)
    strict = ccol < crow
    incl = ccol <= crow
    diag = ccol == crow
    n_pow = int(math.log2(chunk))

    n_chunks = tc // chunk
    n_groups = RWKV_HEADS // group
    lanes = [slice(g * gw, (g + 1) * gw) for g in range(n_groups)]
    chains = [(c, g) for c in range(n_chunks) for g in range(n_groups)]

    a3, a3s, bb3, kb3, v3, kgbg3, v_c, g_end_c = {}, {}, {}, {}, {}, {}, {}, {}
    for c in range(n_chunks):
        sl = slice(c * chunk, (c + 1) * chunk)
        r, kpc, v, kkc, bq, ld = r_t[sl], kp[sl], v_t[sl], kk[sl], bq_t[sl], ld_t[sl]
        cum = ld
        d = 1
        while d < chunk:
            cum = cum + _shift_rows(cum, d, 0.0)
            d *= 2
        cum_end = cum[chunk - 1:chunk, :]
        inv = jnp.exp(-cum)
        tail_decay = jnp.exp(cum_end - cum)
        rt = r * jnp.exp(cum)
        kkt = kkc * jnp.exp(cum - ld)
        kb = kpc * inv
        bb = bq * inv
        kg = kpc * tail_decay
        bg = bq * tail_decay
        g_end_c[c] = jnp.exp(cum_end)
        for g, ls in enumerate(lanes):
            a_in = jnp.concatenate([kkt[:, ls], rt[:, ls]], axis=0)
            a3[c, g] = _lhs3(a_in, 1, PASSES_INTRA)
            a3s[c, g] = a3[c, g] if PASSES_STATE == PASSES_INTRA else _lhs3(a_in, 1, PASSES_STATE)
            bb3[c, g] = _rhs3(bb[:, ls], 1, group, PASSES_INTRA)
            kb3[c, g] = _rhs3(kb[:, ls], 1, group, PASSES_INTRA)
            v3[c, g] = _rhs3(v[:, ls], 0, group, PASSES_INTRA)
            kgbg3[c, g] = _rhs3(jnp.concatenate([kg[:, ls], -bg[:, ls]], axis=0), 0, 1, PASSES_STATE)
            v_c[c, g] = v[:, ls]

    xb = {ch: _dg(a3[ch], bb3[ch], 1, 1) for ch in chains}
    xk = {ch: _dg(a3[ch], kb3[ch], 1, 1) for ch in chains}
    n = {ch: jnp.where(strict, -xb[ch][0:chunk], 0.0) for ch in chains}
    rb3 = {ch: _lhs3(jnp.where(incl, xb[ch][chunk:], 0.0), 1, PASSES_OUT) for ch in chains}
    lv = {ch: _dg(_lhs3(jnp.concatenate([jnp.where(strict, xk[ch][0:chunk], 0.0),
                                          jnp.where(incl, xk[ch][chunk:], 0.0)], axis=0), 1, PASSES_INTRA),
                  v3[ch], 1, 0) for ch in chains}
    t = {ch: jnp.where(diag, 1.0, n[ch]) for ch in chains}
    p = {ch: _dg(_lhs3(n[ch], 1, PASSES_INV), _rhs3(n[ch], 0, group, PASSES_INV), 1, 0) for ch in chains}
    for q in range(1, n_pow):
        last = q + 1 == n_pow
        passes = PASSES_INV if q <= INV_ACCURATE_ROUNDS else 1
        tp = {ch: _dg(_lhs3(t[ch] if last else jnp.concatenate([t[ch], p[ch]], axis=0), 1, passes),
                      _rhs3(p[ch], 0, group, passes), 1, 0) for ch in chains}
        t = {ch: t[ch] + tp[ch][0:chunk] for ch in chains}
        if not last:
            p = {ch: tp[ch][chunk:] for ch in chains}
    t3 = {ch: _lhs3(t[ch], 1, PASSES_STATE) for ch in chains}

    inv_n = 1.0 / RWKV_HEAD
    bonus = head_sum(r_t * kp * rk_ref[...]) * v_t

    def chunk_out(c, y):
        sl = slice(c * chunk, (c + 1) * chunk)
        yc = y - head_sum(y) * inv_n
        var = head_sum(yc * yc) * inv_n
        yn = yc * lax.rsqrt(var + RWKV_GN_EPS) * lnw_ref[...] + lnb_ref[...]
        return (yn + bonus[sl]) * g_t[sl]

    yb_rows = []
    y_prev = None
    for c in range(n_chunks):
        s_prev = [s_o[:, ls] for ls in lanes]
        xs = [_dg(a3s[c, g], _rhs3(s_prev[g], 1, group, PASSES_STATE), 1, 1) for g in range(n_groups)]
        if y_prev is not None:
            yb_rows.append(chunk_out(c - 1, y_prev))
        fill()
        u = [_dg(t3[c, g], _rhs3(xs[g][0:chunk] + lv[c, g][0:chunk], 0, group, PASSES_STATE), 1, 0)
             for g in range(n_groups)]
        fill()
        ru = [_dg(rb3[c, g], _rhs3(u[g], 0, group, PASSES_OUT), 1, 0) for g in range(n_groups)]
        upd = [_dg(_lhs3(jnp.concatenate([v_c[c, g], u[g]], axis=0), 0, PASSES_STATE), kgbg3[c, g], 0, 0)
               for g in range(n_groups)]
        y_prev = jnp.concatenate(
            [xs[g][chunk:] + lv[c, g][chunk:] - ru[g] for g in range(n_groups)], axis=1)
        for g, ls in enumerate(lanes):
            s_o[:, ls] = s_prev[g] * g_end_c[c][:, ls] + _fold_diag(upd[g], group)
    yb_rows.append(chunk_out(n_chunks - 1, y_prev))
    y_b = jnp.concatenate(yb_rows, axis=0)

    for _ in range(5):
        fill()
    mix = _mm(jnp.concatenate([branch["y_a"], y_b, branch["y_c"]], axis=-1), w_out_ref[...])
    xo_ref[...] = x + mod_ref[2:3, :] * mix


def _lag_shift_matrix(tc):
    t = jnp.arange(tc)
    blocks = [((t[:, None] - d == t[None, :]) & ((t % S5_SUB) >= d)[:, None]) for d in range(1, S5_SUB)]
    return jnp.concatenate(blocks, axis=0).astype(BF16)


def _const_spec(arr):
    nd = arr.ndim
    return pl.BlockSpec(arr.shape, lambda b, i: (0,) * nd)


def _batch_spec(arr):
    nd = arr.ndim
    return pl.BlockSpec((None,) + arr.shape[1:], lambda b, i: (b,) + (0,) * (nd - 1))


def _mixer(tl, x, mod, lp, st):
    bsz, seq, _ = x.shape
    tc = tl.t_mix
    consts1 = [lp["norm_mix"], lp["w_in"], lp["w_out"],
               lp["lru_conv_w"], lp["lru_conv_b"], lp["lru_wa"], lp["lru_ba"], lp["lru_wx"],
               lp["lru_bx"], lp["lru_lambda"]]
    states1 = [st["lru_buf"], st["lru_h"]]
    consts2 = [lp["rwkv_mu"], lp["rwkv_w0"], lp["rwkv_w2"], lp["rwkv_a0"], lp["rwkv_a2"],
               lp["rwkv_g2"], lp["rwkv_k_k"], lp["rwkv_k_a"], lp["rwkv_r_k"], lp["rwkv_ln_w"],
               lp["rwkv_ln_b"], lp["ones_bd"]]
    states2 = [st["rwkv_shift"], st["rwkv_s"]]
    sub_rows = max(tc // S5_SUB, SUBLANES)
    sub_sum = (jnp.arange(tc)[None, :] // S5_SUB == jnp.arange(sub_rows)[:, None]).astype(BF16)
    consts3 = list(lp["s5_tables"]) + [_lag_shift_matrix(tc), sub_sum, lp["s5_c_re"], lp["s5_c_im"],
                                       lp["s5_d"], lp["s5_glu_w"], lp["s5_glu_b"]]
    states3 = [st["s5_re"], st["s5_im"]]
    x_spec = pl.BlockSpec((None, tc, D_MODEL), lambda b, i: (b, i, 0))
    in_specs = ([x_spec, _batch_spec(mod)]
                + [_const_spec(a) for a in consts1] + [_batch_spec(a) for a in states1]
                + [_const_spec(a) for a in consts2] + [_batch_spec(a) for a in states2]
                + [_const_spec(a) for a in consts3] + [_batch_spec(a) for a in states3])
    state_list = states1 + states2 + states3
    out_shape = [jax.ShapeDtypeStruct(x.shape, F32)] + [jax.ShapeDtypeStruct(a.shape, F32) for a in state_list]
    out_specs = [x_spec] + [_batch_spec(a) for a in state_list]
    outs = pl.pallas_call(
        functools.partial(_mixer_kernel, tl),
        grid=(bsz, seq // tc),
        in_specs=in_specs, out_specs=out_specs, out_shape=out_shape,
        compiler_params=pltpu.CompilerParams(
            dimension_semantics=("arbitrary", "arbitrary"), vmem_limit_bytes=VMEM_LIMIT_BYTES),
        name="mixer",
    )(x, mod, *consts1, *states1, *consts2, *states2, *consts3, *states3)
    x_new = outs[0]
    keys = ["lru_buf", "lru_h", "rwkv_shift", "rwkv_s", "s5_re", "s5_im"]
    return x_new, dict(zip(keys, outs[1:]))


def _ffn_kernel(tl, final,
                x_ref, mod_ref, norm_ref, up_ref, cw_ref, cb_ref, down_ref, tail0_ref, fin_ref,
                xo_ref, tail_o):
    tf = tl.t_ffn
    i = pl.program_id(1)

    @pl.when(i == 0)
    def _init():
        tail_o[...] = tail0_ref[...]

    x = x_ref[...]
    h = _rmsnorm_mod(x, norm_ref[...], mod_ref[4:5, :], mod_ref[3:4, :]).astype(BF16)

    def up_block(c0):
        return jnp.dot(h, up_ref[:, c0:c0 + FFN_COL_BLOCK], preferred_element_type=F32)

    def conv_block(up, c0):
        cs = slice(c0, c0 + FFN_COL_BLOCK)
        tail = tail_o[:, cs]
        m1 = _shift_in(up, 1, tail)
        m2 = _shift_in(up, 2, tail)
        tail_o[:, cs] = up[tf - SUBLANES:tf, :]
        return cb_ref[:, cs] + m2 * cw_ref[0:1, cs] + m1 * cw_ref[1:2, cs] + up * cw_ref[2:3, cs]

    blocks = list(range(0, D_FF, FFN_COL_BLOCK))
    pending = (up_block(blocks[0]), up_block(D_FF + blocks[0]))
    acc = None
    acts = []
    for j, c0 in enumerate(blocks):
        up_val, up_gate = pending
        if j + 1 < len(blocks):
            pending = (up_block(blocks[j + 1]), up_block(D_FF + blocks[j + 1]))
        val = conv_block(up_val, c0)
        gate = conv_block(up_gate, D_FF + c0)
        acts.append((val * (gate * jax.nn.sigmoid(gate))).astype(BF16))
        if len(acts) == FFN_DOWN_GROUP or j + 1 == len(blocks):
            r0 = c0 + FFN_COL_BLOCK - len(acts) * FFN_COL_BLOCK
            part = jnp.dot(jnp.concatenate(acts, axis=1), down_ref[r0:c0 + FFN_COL_BLOCK, :],
                           preferred_element_type=F32)
            acc = part if acc is None else acc + part
            acts = []
    y = x + mod_ref[5:6, :] * acc
    if final:
        y = y * lax.rsqrt(jnp.mean(y * y, axis=-1, keepdims=True) + NORM_EPS) * fin_ref[...]
    xo_ref[...] = y


def _ffn(tl, final, x, mod, lp, tail0, norm_final):
    bsz, seq, _ = x.shape
    tf = tl.t_ffn
    consts = [lp["norm_ffn"], lp["ffn_up"], lp["ffn_conv_w"], lp["ffn_conv_b"], lp["ffn_down"]]
    x_spec = pl.BlockSpec((None, tf, D_MODEL), lambda b, i: (b, i, 0))
    in_specs = ([x_spec, _batch_spec(mod)] + [_const_spec(a) for a in consts]
                + [_batch_spec(tail0), _const_spec(norm_final)])
    x_new, tail = pl.pallas_call(
        functools.partial(_ffn_kernel, tl, final),
        grid=(bsz, seq // tf),
        in_specs=in_specs,
        out_specs=[x_spec, _batch_spec(tail0)],
        out_shape=[jax.ShapeDtypeStruct(x.shape, F32), jax.ShapeDtypeStruct(tail0.shape, F32)],
        compiler_params=pltpu.CompilerParams(
            dimension_semantics=("arbitrary", "arbitrary"), vmem_limit_bytes=VMEM_LIMIT_BYTES),
        name="conv_ffn",
    )(x, mod, *consts, tail0, norm_final)
    return x_new, tail


def _block_diag_dense(blocks):
    n, r, c = blocks.shape
    eye = jnp.eye(n, dtype=blocks.dtype)
    return (eye[:, None, :, None] * blocks[:, :, None, :]).reshape(n * r, n * c)


def _layer_params(p, l):
    row = lambda a: a.reshape(1, -1)
    zeros_lora = jnp.zeros((64, RWKV_WIDTH), F32)
    head = jnp.arange(RWKV_WIDTH) // RWKV_HEAD
    lp = {
        "norm_mix": row(p["norm_mix"][l]), "norm_ffn": row(p["norm_ffn"][l]),
        "w_in": p["w_in"][l].astype(BF16), "w_out": p["w_out"][l].astype(BF16),
        "lru_conv_w": p["lru_conv_w"][l], "lru_conv_b": row(p["lru_conv_b"][l]),
        "lru_wa": _block_diag_dense(p["lru_wa"][l]).astype(BF16), "lru_ba": row(p["lru_ba"][l]),
        "lru_wx": _block_diag_dense(p["lru_wx"][l]).astype(BF16), "lru_bx": row(p["lru_bx"][l]),
        "lru_lambda": row(p["lru_lambda"][l]),
        "rwkv_mu": row(p["rwkv_mu"][l]), "rwkv_w0": row(p["rwkv_w0"][l]),
        "rwkv_w2": jnp.concatenate([p["rwkv_w2"][l], zeros_lora], axis=0).astype(BF16),
        "rwkv_a0": row(p["rwkv_a0"][l]),
        "rwkv_a2": jnp.concatenate([zeros_lora, p["rwkv_a2"][l]], axis=0).astype(BF16),
        "rwkv_g2": p["rwkv_g2"][l].astype(BF16),
        "rwkv_k_k": row(p["rwkv_k_k"][l]), "rwkv_k_a": row(p["rwkv_k_a"][l]),
        "rwkv_r_k": row(p["rwkv_r_k"][l]), "rwkv_ln_w": row(p["rwkv_ln_w"][l]),
        "rwkv_ln_b": row(p["rwkv_ln_b"][l]),
        "ones_bd": (head[:256, None] == head[None, :256]).astype(BF16),
        "s5_a_re": row(p["s5_a_re"][l]), "s5_a_im": row(p["s5_a_im"][l]),
        "s5_log_dt": row(jnp.repeat(p["s5_log_dt"][l], 64)),
        "s5_b_re": _block_diag_dense(jnp.swapaxes(p["s5_b_re"][l], 1, 2)),
        "s5_b_im": _block_diag_dense(jnp.swapaxes(p["s5_b_im"][l], 1, 2)),
        "s5_c_re_f32": _block_diag_dense(jnp.swapaxes(p["s5_c_re"][l], 1, 2)),
        "s5_c_im_f32": _block_diag_dense(jnp.swapaxes(p["s5_c_im"][l], 1, 2)),
        "s5_d": row(p["s5_d"][l]),
        "s5_glu_w": p["s5_glu_w"][l].astype(BF16), "s5_glu_b": row(p["s5_glu_b"][l]),
        "ffn_up": p["ffn_up"][l].astype(BF16), "ffn_conv_w": p["ffn_conv_w"][l],
        "ffn_conv_b": row(p["ffn_conv_b"][l]), "ffn_down": p["ffn_down"][l].astype(BF16),
    }
    lp["s5_tables"] = _s5_prep(lp)
    lp["s5_c_re"] = lp["s5_c_re_f32"].astype(BF16)
    lp["s5_c_im"] = lp["s5_c_im_f32"].astype(BF16)
    return lp


def _pad_tail(buf):
    return jnp.pad(buf, ((0, 0), (SUBLANES - buf.shape[1], 0), (0, 0)))


def _state_in(states, l):
    lru_buf, lru_h, rw_shift, rw_s, s5_re, s5_im, ffn_buf = (s[l] for s in states)
    bsz = lru_h.shape[0]
    return {
        "lru_buf": _pad_tail(lru_buf),
        "lru_h": lru_h.reshape(bsz, 1, LRU_WIDTH),
        "rwkv_shift": rw_shift.reshape(bsz, 1, RWKV_COLS),
        "rwkv_s": jnp.transpose(rw_s, (0, 2, 1, 3)).reshape(bsz, RWKV_HEAD, RWKV_WIDTH),
        "s5_re": s5_re.reshape(bsz, 1, S5_STATES),
        "s5_im": s5_im.reshape(bsz, 1, S5_STATES),
    }, _pad_tail(ffn_buf)


def _state_out(st, ffn_tail):
    bsz = st["lru_h"].shape[0]
    return (st["lru_buf"][:, SUBLANES - (LRU_CONV - 1):, :],
            st["lru_h"].reshape(bsz, LRU_WIDTH),
            st["rwkv_shift"].reshape(bsz, RWKV_COLS),
            jnp.transpose(st["rwkv_s"].reshape(bsz, RWKV_HEAD, RWKV_HEADS, RWKV_HEAD), (0, 2, 1, 3)),
            st["s5_re"].reshape(bsz, 16, 64),
            st["s5_im"].reshape(bsz, 16, 64),
            ffn_tail[:, SUBLANES - 2:, :])


def _trunk(x, mods, states, layer_params, norm_final):
    depth = len(layer_params)
    tl = _tiling(x.shape[1])
    new = []
    for l in range(depth):
        st, ffn_tail0 = _state_in(states, l)
        x, st = _mixer(tl, x, mods[l], layer_params[l], st)
        x, ffn_tail = _ffn(tl, l == depth - 1, x, mods[l], layer_params[l], ffn_tail0, norm_final)
        new.append(_state_out(st, ffn_tail))
    stacked = tuple(jnp.stack([n[j] for n in new], axis=0) for j in range(7))
    return x, stacked


def _zero_states(depth, bsz):
    return (jnp.zeros((depth, bsz, LRU_CONV - 1, LRU_WIDTH), F32),
            jnp.zeros((depth, bsz, LRU_WIDTH), F32),
            jnp.zeros((depth, bsz, RWKV_COLS), F32),
            jnp.zeros((depth, bsz, RWKV_HEADS, RWKV_HEAD, RWKV_HEAD), F32),
            jnp.zeros((depth, bsz, 16, 64), F32),
            jnp.zeros((depth, bsz, 16, 64), F32),
            jnp.zeros((depth, bsz, 2, 2 * D_FF), F32))


def kernel(x_prompt, x_sample, c_prompt, c_sample, state_lru_conv, state_lru_h, state_rwkv_shift, state_rwkv_S, state_s5_re, state_s5_im, state_ffn_conv, w_ada, b_ada, norm_mix, norm_ffn, w_in, w_out, lru_conv_w, lru_conv_b, lru_wa, lru_ba, lru_wx, lru_bx, lru_lambda, rwkv_mu, rwkv_w0, rwkv_w2, rwkv_a0, rwkv_a2, rwkv_g2, rwkv_k_k, rwkv_k_a, rwkv_r_k, rwkv_ln_w, rwkv_ln_b, s5_a_re, s5_a_im, s5_b_re, s5_b_im, s5_c_re, s5_c_im, s5_d, s5_log_dt, s5_glu_w, s5_glu_b, ffn_up, ffn_conv_w, ffn_conv_b, ffn_down, norm_final):
    p = dict(norm_mix=norm_mix, norm_ffn=norm_ffn, w_in=w_in, w_out=w_out,
             lru_conv_w=lru_conv_w, lru_conv_b=lru_conv_b, lru_wa=lru_wa, lru_ba=lru_ba,
             lru_wx=lru_wx, lru_bx=lru_bx, lru_lambda=lru_lambda,
             rwkv_mu=rwkv_mu, rwkv_w0=rwkv_w0, rwkv_w2=rwkv_w2, rwkv_a0=rwkv_a0, rwkv_a2=rwkv_a2,
             rwkv_g2=rwkv_g2, rwkv_k_k=rwkv_k_k, rwkv_k_a=rwkv_k_a,
             rwkv_r_k=rwkv_r_k.reshape(rwkv_r_k.shape[0], RWKV_WIDTH),
             rwkv_ln_w=rwkv_ln_w, rwkv_ln_b=rwkv_ln_b,
             s5_a_re=s5_a_re, s5_a_im=s5_a_im, s5_b_re=s5_b_re, s5_b_im=s5_b_im,
             s5_c_re=s5_c_re, s5_c_im=s5_c_im, s5_d=s5_d, s5_log_dt=s5_log_dt,
             s5_glu_w=s5_glu_w, s5_glu_b=s5_glu_b,
             ffn_up=ffn_up, ffn_conv_w=ffn_conv_w, ffn_conv_b=ffn_conv_b, ffn_down=ffn_down)
    depth = w_in.shape[0]
    layer_params = [_layer_params(p, l) for l in range(depth)]
    fin = norm_final.reshape(1, D_MODEL)
    n_prompt = c_prompt.shape[0]
    mod = _ada(jnp.concatenate([c_prompt, c_sample], axis=0), w_ada, b_ada)
    mod = mod.reshape(depth, mod.shape[1], 6, D_MODEL)
    mods_p = [mod[l, :n_prompt] for l in range(depth)]
    mods_s = [mod[l, n_prompt:] for l in range(depth)]
    s_in = (state_lru_conv, state_lru_h, state_rwkv_shift, state_rwkv_S, state_s5_re, state_s5_im,
            state_ffn_conv)
    y_sample, s_states = _trunk(x_sample, mods_s, s_in, layer_params, fin)
    y_prompt, p_states = _trunk(x_prompt, mods_p, _zero_states(depth, x_prompt.shape[0]), layer_params, fin)
    return (y_prompt, y_sample) + tuple(p_states) + tuple(s_states)
```

```python
import functools
import math
from typing import NamedTuple

import jax
import jax.numpy as jnp
from jax import lax
from jax.experimental import pallas as pl
from jax.experimental.pallas import tpu as pltpu

F32 = jnp.float32
BF16 = jnp.bfloat16

D_MODEL = 1024
LRU_WIDTH = 256
LRU_CONV = 4
LRU_C = 8.0
RWKV_WIDTH = 512
RWKV_HEAD = 64
RWKV_HEADS = 8
RWKV_COLS = 1792
S5_WIDTH = 256
S5_STATES = 1024
S5_POW_ROWS = 8
S5_SUB = 8
D_FF = 2816
NORM_EPS = 1e-6
RWKV_GN_EPS = 64e-5
SUBLANES = 8
FFN_COL_BLOCK = 256
FFN_DOWN_GROUP = 4
VMEM_LIMIT_BYTES = 56 * 1024 * 1024
PASSES_INTRA = 1
PASSES_INV = 3
INV_ACCURATE_ROUNDS = 3
PASSES_STATE = 1
PASSES_OUT = 1


class Tiling(NamedTuple):
    t_mix: int
    chunk: int
    group: int
    t_ffn: int


def _tiling(seq_len):
    t_mix = min(seq_len, 256)
    chunk = min(t_mix, 64)
    group = min(RWKV_HEADS, 256 // chunk)
    return Tiling(t_mix, chunk, group, min(seq_len, 512))


def _dg(a, b, ca, cb):
    return lax.dot_general(a, b, (((ca,), (cb,)), ((), ())), preferred_element_type=F32)


def _mm(a, w):
    return jnp.dot(a.astype(BF16), w, preferred_element_type=F32)


def _expm1(x):
    u = jnp.exp(x)
    small = u >= 0.5
    stable = (u - 1.0) * x / jnp.log(jnp.where(small, u, 0.75))
    return jnp.where(u == 1.0, x, jnp.where(small, stable, u - 1.0))


def _shift_rows(x, d, fill):
    if d % SUBLANES == 0:
        return jnp.concatenate([jnp.full((d, x.shape[1]), fill, x.dtype), x[:x.shape[0] - d]], axis=0)
    rolled = pltpu.roll(x, d, axis=0)
    if x.shape[0] <= SUBLANES:
        return jnp.where(lax.broadcasted_iota(jnp.int32, x.shape, 0) >= d, rolled, fill)
    r8 = lax.broadcasted_iota(jnp.int32, (SUBLANES, 1), 0)
    return jnp.concatenate([jnp.where(r8 >= d, rolled[0:SUBLANES], fill), rolled[SUBLANES:]], axis=0)


def _shift_in(x, d, tail):
    rolled = pltpu.roll(x, d, axis=0)
    head = rolled[0:SUBLANES]
    r8 = lax.broadcasted_iota(jnp.int32, (SUBLANES, 1), 0)
    for j in range(d):
        head = jnp.where(r8 == j, tail[SUBLANES - d + j:SUBLANES - d + j + 1], head)
    return jnp.concatenate([head, rolled[SUBLANES:]], axis=0)


def _linear_scan(a, b):
    d = 1
    while d < a.shape[0]:
        b = a * _shift_rows(b, d, 0.0) + b
        a = a * _shift_rows(a, d, 1.0)
        d *= 2
    return b


def _block_mask(rows, cols, rb, cb, dtype):
    r = lax.broadcasted_iota(jnp.int32, (rows, cols), 0) >> int(math.log2(rb))
    c = lax.broadcasted_iota(jnp.int32, (rows, cols), 1) >> int(math.log2(cb))
    return jnp.where(r == c, 1.0, 0.0).astype(dtype)


def _split(x):
    hi = x.astype(BF16)
    return hi, (x - hi.astype(F32)).astype(BF16)


def _lhs3(a, axis, passes=3):
    if passes == 1:
        return a.astype(BF16)
    hi, lo = _split(a)
    return jnp.concatenate([hi, hi, lo], axis=axis)


def _rhs3(b, axis, group=1, passes=3):
    parts = (b.astype(BF16),) if passes == 1 else _split(b)
    if group > 1:
        rows, cols = b.shape
        mask = _block_mask(group * rows, cols, rows, cols // group, BF16)
        parts = tuple(jnp.concatenate([q] * group, axis=0) * mask for q in parts)
    if passes == 1:
        return parts[0]
    return jnp.concatenate([parts[0], parts[1], parts[0]], axis=axis)


def _fold_diag(m, group):
    rows = m.shape[0] // group
    width = m.shape[1] // group
    rb = lax.broadcasted_iota(jnp.int32, m.shape, 0) >> int(math.log2(rows))
    cb = lax.broadcasted_iota(jnp.int32, m.shape, 1) >> int(math.log2(width))
    m = jnp.where(rb == cb, m, 0.0)
    out = m[0:rows]
    for h in range(1, group):
        out = out + m[h * rows:(h + 1) * rows]
    return out


def _rmsnorm_mod(x, gain, scale, shift):
    y = x * lax.rsqrt(jnp.mean(x * x, axis=-1, keepdims=True) + NORM_EPS)
    return (y * gain) * (1.0 + scale) + shift


def _ada_kernel(c_ref, w_ref, b_ref, o_ref):
    c = c_ref[...]
    o_ref[...] = _mm(c * jax.nn.sigmoid(c), w_ref[...].astype(BF16)) + b_ref[...]


def _ada(c, w_ada, b_ada):
    depth, _, n = w_ada.shape
    rows = c.shape[0]
    tn = 1536
    return pl.pallas_call(
        _ada_kernel,
        grid=(depth, n // tn),
        in_specs=[pl.BlockSpec((rows, D_MODEL), lambda l, j: (0, 0)),
                  pl.BlockSpec((None, D_MODEL, tn), lambda l, j: (l, 0, j)),
                  pl.BlockSpec((None, 1, tn), lambda l, j: (l, 0, j))],
        out_specs=pl.BlockSpec((None, rows, tn), lambda l, j: (l, 0, j)),
        out_shape=jax.ShapeDtypeStruct((depth, rows, n), F32),
        compiler_params=pltpu.CompilerParams(
            dimension_semantics=("arbitrary", "arbitrary"), vmem_limit_bytes=VMEM_LIMIT_BYTES),
        name="ada_mod",
    )(c, w_ada, b_ada.reshape(depth, 1, n))


def _s5_prep_kernel(are_ref, aim_ref, ldt_ref, bre_ref, bim_ref, cre_ref, cim_ref,
                    bbar_re, bbar_im, pw_re, pw_im, lagk, tab_end_re, tab_end_im, tab_in_re, tab_in_im):
    dt = jnp.exp(ldt_ref[...])
    a_re = are_ref[...]
    a_im = aim_ref[...]
    mag = jnp.exp(dt * a_re)
    abr = mag * jnp.cos(dt * a_im)
    abi = mag * jnp.sin(dt * a_im)
    den = a_re * a_re + a_im * a_im
    fr = ((abr - 1.0) * a_re + abi * a_im) / den
    fi = (abi * a_re - (abr - 1.0) * a_im) / den
    bb_re = fr * bre_ref[...] - fi * bim_ref[...]
    bb_im = fr * bim_ref[...] + fi * bre_ref[...]
    bbar_re[...] = bb_re.astype(BF16)
    bbar_im[...] = bb_im.astype(BF16)
    pr, pi = abr, abi
    for k in range(S5_POW_ROWS):
        pw_re[k:k + 1, :] = pr
        pw_im[k:k + 1, :] = pi
        pr, pi = pr * pr - pi * pi, 2.0 * pr * pi
    c3_re = _rhs3(cre_ref[...], 0)
    c3_im = _rhs3(cim_ref[...], 0)
    pr, pi = jnp.ones_like(abr), jnp.zeros_like(abi)
    for d in range(S5_SUB + 1):
        if d < S5_SUB:
            lagk[d * S5_WIDTH:(d + 1) * S5_WIDTH, :] = (
                _dg(_lhs3(bb_re * pr - bb_im * pi, 1), c3_re, 1, 0)
                - _dg(_lhs3(bb_re * pi + bb_im * pr, 1), c3_im, 1, 0)).astype(BF16)
            tab_end_re[S5_SUB - 1 - d:S5_SUB - d, :] = pr
            tab_end_im[S5_SUB - 1 - d:S5_SUB - d, :] = pi
        if d > 0:
            tab_in_re[d - 1:d, :] = pr
            tab_in_im[d - 1:d, :] = pi
        pr, pi = pr * abr - pi * abi, pr * abi + pi * abr


def _s5_prep(lp):
    ins = [lp["s5_a_re"], lp["s5_a_im"], lp["s5_log_dt"], lp["s5_b_re"], lp["s5_b_im"],
           lp["s5_c_re_f32"], lp["s5_c_im_f32"]]
    return pl.pallas_call(
        _s5_prep_kernel,
        out_shape=[jax.ShapeDtypeStruct((S5_WIDTH, S5_STATES), BF16)] * 2
        + [jax.ShapeDtypeStruct((S5_POW_ROWS, S5_STATES), F32)] * 2
        + [jax.ShapeDtypeStruct((S5_SUB * S5_WIDTH, S5_WIDTH), BF16)]
        + [jax.ShapeDtypeStruct((S5_SUB, S5_STATES), F32)] * 4,
        compiler_params=pltpu.CompilerParams(vmem_limit_bytes=VMEM_LIMIT_BYTES),
        name="s5_prep",
    )(*ins)


def _mixer_kernel(tl,
                  x_ref, mod_ref, norm_ref, w_in_ref, w_out_ref,
                  conv_w_ref, conv_b_ref, wa_ref, ba_ref, wx_ref, bx_ref, lam_ref,
                  lru_buf0_ref, lru_h0_ref,
                  mu_ref, w0_ref, w2_ref, a0_ref, a2_ref, g2_ref, kk_ref, ka_ref, rk_ref,
                  lnw_ref, lnb_ref, ones_ref, shift0_ref, s0_ref,
                  bbar_re, bbar_im, pw_re, pw_im, lagk_ref, tab_end_re, tab_end_im, tab_in_re, tab_in_im,
                  lag_shift_ref, sub_sum_ref, cre_ref, cim_ref, d_ref,
                  gluw_ref, glub_ref, s5re0_ref, s5im0_ref,
                  xo_ref, lru_buf_o, lru_h_o, shift_o, s_o, s5re_o, s5im_o):
    tc, chunk, group = tl.t_mix, tl.chunk, tl.group
    assert int(math.log2(S5_SUB)) + int(math.log2(max(tc // S5_SUB, 2))) <= S5_POW_ROWS
    i = pl.program_id(1)

    @pl.when(i == 0)
    def _init():
        lru_buf_o[...] = lru_buf0_ref[...]
        lru_h_o[...] = lru_h0_ref[...]
        shift_o[...] = shift0_ref[...]
        s_o[...] = s0_ref[...]
        s5re_o[...] = s5re0_ref[...]
        s5im_o[...] = s5im0_ref[...]

    x = x_ref[...]
    r8 = lax.broadcasted_iota(jnp.int32, (SUBLANES, 1), 0)
    h = _rmsnorm_mod(x, norm_ref[...], mod_ref[1:2, :], mod_ref[0:1, :])
    proj = _mm(h, w_in_ref[...])
    u_gate = proj[:, 0:256]
    u_lru = proj[:, 256:512]
    p_rw = proj[:, 512:2304]
    u_s5 = proj[:, 2304:2560]

    branch = {}

    def lru_step():
        cw = conv_w_ref[...]
        lru_tail = lru_buf_o[...]
        xc = conv_b_ref[...] + _shift_in(u_lru, LRU_CONV - 1, lru_tail) * cw[0:1]
        for k in range(1, LRU_CONV - 1):
            xc = xc + _shift_in(u_lru, LRU_CONV - 1 - k, lru_tail) * cw[k:k + 1]
        xc = xc + u_lru * cw[LRU_CONV - 1:LRU_CONV]
        lru_buf_o[...] = u_lru[tc - SUBLANES:tc, :]
        r_gate = jax.nn.sigmoid(_mm(xc, wa_ref[...]) + ba_ref[...])
        i_gate = jax.nn.sigmoid(_mm(xc, wx_ref[...]) + bx_ref[...])
        log_a = -LRU_C * r_gate * jax.nn.softplus(-lam_ref[...])
        a = jnp.exp(log_a)
        gain = jnp.sqrt(jnp.maximum(-_expm1(2.0 * log_a), 0.0))
        b = gain * i_gate * xc
        b = jnp.concatenate([b[0:SUBLANES] + jnp.where(r8 == 0, a[0:SUBLANES] * lru_h_o[...], 0.0),
                             b[SUBLANES:]], axis=0)
        h_lru = _linear_scan(a, b)
        lru_h_o[...] = h_lru[tc - 1:tc, :]
        branch["y_a"] = h_lru * jax.nn.gelu(u_gate)

    n_sub = tc // S5_SUB
    u_bf = u_s5.astype(BF16)

    def s5_lag_step():
        lag_shift = lag_shift_ref[...]
        lb = lag_shift.shape[1]
        rows = []
        for r0 in range(0, tc, lb):
            u_blk = u_bf[r0:r0 + lb]
            lagged = jnp.dot(lag_shift, u_blk, preferred_element_type=F32)
            rows.append(jnp.concatenate(
                [u_blk] + [lagged[(d - 1) * lb:d * lb].astype(BF16) for d in range(1, S5_SUB)], axis=1))
        branch["lags"] = jnp.concatenate(rows, axis=0)

    def s5_intra_step():
        branch["y_intra"] = jnp.dot(branch["lags"], lagk_ref[...], preferred_element_type=F32)

    def s5_state_step():
        bu_re = _mm(u_bf, bbar_re[...])
        bu_im = _mm(u_bf, bbar_im[...])
        te_re = jnp.concatenate([tab_end_re[...]] * n_sub, axis=0)
        te_im = jnp.concatenate([tab_end_im[...]] * n_sub, axis=0)
        sub_sum = sub_sum_ref[...]
        e_re = _mm(sub_sum, (te_re * bu_re - te_im * bu_im).astype(BF16))[0:n_sub]
        e_im = _mm(sub_sum, (te_re * bu_im + te_im * bu_re).astype(BF16))[0:n_sub]
        h0r = s5re_o[...]
        h0i = s5im_o[...]
        k_sub = int(math.log2(S5_SUB))
        ar = pw_re[k_sub:k_sub + 1, :]
        ai = pw_im[k_sub:k_sub + 1, :]
        if n_sub <= SUBLANES:
            cr, ci = h0r, h0i
            prev_r, prev_i = [], []
            for n in range(n_sub):
                prev_r.append(jnp.broadcast_to(cr, (S5_SUB, S5_STATES)))
                prev_i.append(jnp.broadcast_to(ci, (S5_SUB, S5_STATES)))
                cr, ci = ar * cr - ai * ci + e_re[n:n + 1], ar * ci + ai * cr + e_im[n:n + 1]
            s5re_o[...] = cr
            s5im_o[...] = ci
            pvr = jnp.concatenate(prev_r, axis=0)
            pvi = jnp.concatenate(prev_i, axis=0)
        else:
            rn = lax.broadcasted_iota(jnp.int32, (n_sub, 1), 0)
            hr = e_re + jnp.where(rn == 0, ar * h0r - ai * h0i, 0.0)
            hi = e_im + jnp.where(rn == 0, ar * h0i + ai * h0r, 0.0)
            d = 1
            while d < n_sub:
                k = k_sub + int(math.log2(d))
                pr = pw_re[k:k + 1, :]
                pi = pw_im[k:k + 1, :]
                sr = _shift_rows(hr, d, 0.0)
                si = _shift_rows(hi, d, 0.0)
                hr, hi = hr + (pr * sr - pi * si), hi + (pr * si + pi * sr)
                d *= 2
            s5re_o[...] = hr[n_sub - 1:n_sub, :]
            s5im_o[...] = hi[n_sub - 1:n_sub, :]
            pvr = jnp.where(rn == 0, h0r, pltpu.roll(hr, 1, axis=0))
            pvi = jnp.where(rn == 0, h0i, pltpu.roll(hi, 1, axis=0))
            pvr = jnp.broadcast_to(pvr[:, None, :], (n_sub, S5_SUB, S5_STATES)).reshape(tc, S5_STATES)
            pvi = jnp.broadcast_to(pvi[:, None, :], (n_sub, S5_SUB, S5_STATES)).reshape(tc, S5_STATES)
        branch["pv"] = (pvr, pvi)

    def s5_out_step():
        pvr, pvi = branch["pv"]
        ti_re = jnp.concatenate([tab_in_re[...]] * n_sub, axis=0)
        ti_im = jnp.concatenate([tab_in_im[...]] * n_sub, axis=0)
        y_inter = (_mm(ti_re * pvr - ti_im * pvi, cre_ref[...])
                   - _mm(ti_re * pvi + ti_im * pvr, cim_ref[...]))
        y5 = branch["y_intra"] + y_inter + d_ref[...] * u_s5
        z = jax.nn.gelu(y5)
        branch["y_c"] = z * jax.nn.sigmoid(_mm(z, gluw_ref[...]) + glub_ref[...])

    fillers = iter([s5_lag_step, lru_step, s5_intra_step, s5_state_step, s5_out_step])

    def fill():
        step = next(fillers, None)
        if step is not None:
            step()

    rolled = pltpu.roll(p_rw, 1, axis=0)
    prev =jnp.concatenate([jnp.where(r8 == 0, shift_o[...], rolled[0:SUBLANES]), rolled[SUBLANES:]], axis=0)
    shift_o[...] = p_rw[tc - 1:tc, :]
    ones_bd = ones_ref[...]

    def head_sum(z):
        w = ones_bd.shape[0]
        return jnp.concatenate([_mm(z[:, j:j + w], ones_bd) for j in range(0, RWKV_WIDTH, w)], axis=1)

    def rwkv_operands(rows):
        xm = p_rw[rows] + (prev[rows] - p_rw[rows]) * mu_ref[...]
        r_t = xm[:, 0:512]
        k_t = xm[:, 512:1024]
        v_t = xm[:, 1024:1536]
        lo = xm[:, 1536:1664]
        g_lo = xm[:, 1664:1792]
        log_w = -jax.nn.softplus(-(w0_ref[...] + _mm(jnp.tanh(lo), w2_ref[...]))) - 0.5
        a_t = jax.nn.sigmoid(a0_ref[...] + _mm(lo, a2_ref[...]))
        g_t = _mm(jax.nn.sigmoid(g_lo), g2_ref[...])
        kk = k_t * kk_ref[...]
        kk = kk * lax.rsqrt(jnp.maximum(head_sum(kk * kk), 1e-24))
        kp = k_t * (1.0 + (a_t - 1.0) * ka_ref[...])
        bonus = head_sum(r_t * kp * rk_ref[...]) * v_t
        ld_t = -jnp.exp(log_w)
        return r_t, kp, v_t, kk, kk * a_t, ld_t, g_t, bonus

    gw = group * RWKV_HEAD
    crow = lax.broadcasted_iota(jnp.int32, (chunk, group * chunk), 0)
    ccol = lax.broadcasted_iota(jnp.int32, (chunk, group * chunk), 1) & (chunk - 1)
    strict = ccol < crow
    incl = ccol <= crow
    diag = ccol == crow
    n_pow = int(math.log2(chunk))

    n_chunks = tc // chunk
    n_groups = RWKV_HEADS // group
    lanes = [slice(g * gw, (g + 1) * gw) for g in range(n_groups)]

    a3s, kgbg3, v_c, g_end_c, lv, rb3, t3, gate_c, bonus_c = {}, {}, {}, {}, {}, {}, {}, {}, {}

    def prepare_chunks(c0, c1):
        ops = rwkv_operands(slice(c0 * chunk, c1 * chunk))
        chains = [(c, g) for c in range(c0, c1) for g in range(n_groups)]
        a3, bb3, kb3, v3 = {}, {}, {}, {}
        for c in range(c0, c1):
            sl = slice((c - c0) * chunk, (c - c0 + 1) * chunk)
            r, kpc, v, kkc, bq, ld, gate_c[c], bonus_c[c] = (z[sl] for z in ops)
            _stage_chunk(c, r, kpc, v, kkc, bq, ld, a3, bb3, kb3, v3)
        _solve_chunks(chains, a3, bb3, kb3, v3)

    def _stage_chunk(c, r, kpc, v, kkc, bq, ld, a3, bb3, kb3, v3):
        cum = ld
        d = 1
        while d < chunk:
            cum = cum + _shift_rows(cum, d, 0.0)
            d *= 2
        cum_end = cum[chunk - 1:chunk, :]
        inv = jnp.exp(-cum)
        tail_decay = jnp.exp(cum_end - cum)
        rt = r * jnp.exp(cum)
        kkt = kkc * jnp.exp(cum - ld)
        kb = kpc * inv
        bb = bq * inv
        kg = kpc * tail_decay
        bg = bq * tail_decay
        g_end_c[c] = jnp.exp(cum_end)
        for g, ls in enumerate(lanes):
            a_in = jnp.concatenate([kkt[:, ls], rt[:, ls]], axis=0)
            a3[c, g] = _lhs3(a_in, 1, PASSES_INTRA)
            a3s[c, g] = a3[c, g] if PASSES_STATE == PASSES_INTRA else _lhs3(a_in, 1, PASSES_STATE)
            bb3[c, g] = _rhs3(bb[:, ls], 1, group, PASSES_INTRA)
            kb3[c, g] = _rhs3(kb[:, ls], 1, group, PASSES_INTRA)
            v3[c, g] = _rhs3(v[:, ls], 0, group, PASSES_INTRA)
            kgbg3[c, g] = _rhs3(jnp.concatenate([kg[:, ls], -bg[:, ls]], axis=0), 0, 1, PASSES_STATE)
            v_c[c, g] = v[:, ls]

    def _solve_chunks(chains, a3, bb3, kb3, v3):
        xb = {ch: _dg(a3[ch], bb3[ch], 1, 1) for ch in chains}
        xk = {ch: _dg(a3[ch], kb3[ch], 1, 1) for ch in chains}
        n = {ch: jnp.where(strict, -xb[ch][0:chunk], 0.0) for ch in chains}
        for ch in chains:
            rb3[ch] = _lhs3(jnp.where(incl, xb[ch][chunk:], 0.0), 1, PASSES_OUT)
            lv[ch] = _dg(_lhs3(jnp.concatenate([jnp.where(strict, xk[ch][0:chunk], 0.0),
                                                 jnp.where(incl, xk[ch][chunk:], 0.0)], axis=0), 1, PASSES_INTRA),
                         v3[ch], 1, 0)
        t = {ch: jnp.where(diag, 1.0, n[ch]) for ch in chains}
        p = {ch: _dg(_lhs3(n[ch], 1, PASSES_INV), _rhs3(n[ch], 0, group, PASSES_INV), 1, 0) for ch in chains}
        for q in range(1, n_pow):
            last = q + 1 == n_pow
            passes = PASSES_INV if q <= INV_ACCURATE_ROUNDS else 1
            tp = {ch: _dg(_lhs3(t[ch] if last else jnp.concatenate([t[ch], p[ch]], axis=0), 1, passes),
                          _rhs3(p[ch], 0, group, passes), 1, 0) for ch in chains}
            t = {ch: t[ch] + tp[ch][0:chunk] for ch in chains}
            if not last:
                p = {ch: tp[ch][chunk:] for ch in chains}
        for ch in chains:
            t3[ch] = _lhs3(t[ch], 1, PASSES_STATE)

    prepare_chunks(0, n_chunks)

    inv_n = 1.0 / RWKV_HEAD

    def chunk_out(c, y):
        yc = y - head_sum(y) * inv_n
        var = head_sum(yc * yc) * inv_n
        yn = yc * lax.rsqrt(var + RWKV_GN_EPS) * lnw_ref[...] + lnb_ref[...]
        return (yn + bonus_c[c]) * gate_c[c]

    yb_rows = []
    y_prev = None
    for c in range(n_chunks):
        s_prev = [s_o[:, ls] for ls in lanes]
        xs = [_dg(a3s[c, g], _rhs3(s_prev[g], 1, group, PASSES_STATE), 1, 1) for g in range(n_groups)]
        if y_prev is not None:
            yb_rows.append(chunk_out(c - 1, y_prev))
        fill()
        u = [_dg(t3[c, g], _rhs3(xs[g][0:chunk] + lv[c, g][0:chunk], 0, group, PASSES_STATE), 1, 0)
             for g in range(n_groups)]
        fill()
        ru = [_dg(rb3[c, g], _rhs3(u[g], 0, group, PASSES_OUT), 1, 0) for g in range(n_groups)]
        upd = [_dg(_lhs3(jnp.concatenate([v_c[c, g], u[g]], axis=0), 0, PASSES_STATE), kgbg3[c, g], 0, 0)
               for g in range(n_groups)]
        y_prev = jnp.concatenate(
            [xs[g][chunk:] + lv[c, g][chunk:] - ru[g] for g in range(n_groups)], axis=1)
        for g, ls in enumerate(lanes):
            s_o[:, ls] = s_prev[g] * g_end_c[c][:, ls] + _fold_diag(upd[g], group)
    yb_rows.append(chunk_out(n_chunks - 1, y_prev))
    y_b = jnp.concatenate(yb_rows, axis=0)

    for _ in range(5):
        fill()
    mix = _mm(jnp.concatenate([branch["y_a"], y_b, branch["y_c"]], axis=-1), w_out_ref[...])
    xo_ref[...] = x + mod_ref[2:3, :] * mix


def _lag_shift_matrix(tc):
    t = jnp.arange(tc)
    blocks = [((t[:, None] - d == t[None, :]) & ((t % S5_SUB) >= d)[:, None]) for d in range(1, S5_SUB)]
    return jnp.concatenate(blocks, axis=0).astype(BF16)


def _const_spec(arr):
    nd = arr.ndim
    return pl.BlockSpec(arr.shape, lambda b, i: (0,) * nd)


def _batch_spec(arr):
    nd = arr.ndim
    return pl.BlockSpec((None,) + arr.shape[1:], lambda b, i: (b,) + (0,) * (nd - 1))


def _mixer(tl, x, mod, lp, st):
    bsz, seq, _ = x.shape
    tc = tl.t_mix
    consts1 = [lp["norm_mix"], lp["w_in"], lp["w_out"],
               lp["lru_conv_w"], lp["lru_conv_b"], lp["lru_wa"], lp["lru_ba"], lp["lru_wx"],
               lp["lru_bx"], lp["lru_lambda"]]
    states1 = [st["lru_buf"], st["lru_h"]]
    consts2 = [lp["rwkv_mu"], lp["rwkv_w0"], lp["rwkv_w2"], lp["rwkv_a0"], lp["rwkv_a2"],
               lp["rwkv_g2"], lp["rwkv_k_k"], lp["rwkv_k_a"], lp["rwkv_r_k"], lp["rwkv_ln_w"],
               lp["rwkv_ln_b"], lp["ones_bd"]]
    states2 = [st["rwkv_shift"], st["rwkv_s"]]
    sub_rows = max(tc // S5_SUB, SUBLANES)
    sub_sum = (jnp.arange(tc)[None, :] // S5_SUB == jnp.arange(sub_rows)[:, None]).astype(BF16)
    consts3 = list(lp["s5_tables"]) + [_lag_shift_matrix(min(tc, 256)), sub_sum, lp["s5_c_re"], lp["s5_c_im"],
                                       lp["s5_d"], lp["s5_glu_w"], lp["s5_glu_b"]]
    states3 = [st["s5_re"], st["s5_im"]]
    x_spec = pl.BlockSpec((None, tc, D_MODEL), lambda b, i: (b, i, 0))
    in_specs = ([x_spec, _batch_spec(mod)]
                + [_const_spec(a) for a in consts1] + [_batch_spec(a) for a in states1]
                + [_const_spec(a) for a in consts2] + [_batch_spec(a) for a in states2]
                + [_const_spec(a) for a in consts3] + [_batch_spec(a) for a in states3])
    state_list = states1 + states2 + states3
    out_shape = [jax.ShapeDtypeStruct(x.shape, F32)] + [jax.ShapeDtypeStruct(a.shape, F32) for a in state_list]
    out_specs = [x_spec] + [_batch_spec(a) for a in state_list]
    outs = pl.pallas_call(
        functools.partial(_mixer_kernel, tl),
        grid=(bsz, seq // tc),
        in_specs=in_specs, out_specs=out_specs, out_shape=out_shape,
        compiler_params=pltpu.CompilerParams(
            dimension_semantics=("arbitrary", "arbitrary"), vmem_limit_bytes=VMEM_LIMIT_BYTES),
        name="mixer",
    )(x, mod, *consts1, *states1, *consts2, *states2, *consts3, *states3)
    x_new = outs[0]
    keys = ["lru_buf", "lru_h", "rwkv_shift", "rwkv_s", "s5_re", "s5_im"]
    return x_new, dict(zip(keys, outs[1:]))


def _ffn_kernel(tl, final,
                x_ref, mod_ref, norm_ref, up_ref, cw_ref, cb_ref, down_ref, tail0_ref, fin_ref,
                xo_ref, tail_o):
    tf = tl.t_ffn
    i = pl.program_id(1)

    @pl.when(i == 0)
    def _init():
        tail_o[...] = tail0_ref[...]

    x = x_ref[...]
    h = _rmsnorm_mod(x, norm_ref[...], mod_ref[4:5, :], mod_ref[3:4, :]).astype(BF16)

    def up_block(c0):
        return jnp.dot(h, up_ref[:, c0:c0 + FFN_COL_BLOCK], preferred_element_type=F32)

    def conv_block(up, c0):
        cs = slice(c0, c0 + FFN_COL_BLOCK)
        tail = tail_o[:, cs]
        m1 = _shift_in(up, 1, tail)
        m2 = _shift_in(up, 2, tail)
        tail_o[:, cs] = up[tf - SUBLANES:tf, :]
        return cb_ref[:, cs] + m2 * cw_ref[0:1, cs] + m1 * cw_ref[1:2, cs] + up * cw_ref[2:3, cs]

    blocks = list(range(0, D_FF, FFN_COL_BLOCK))
    pending = (up_block(blocks[0]), up_block(D_FF + blocks[0]))
    acc = None
    acts = []
    for j, c0 in enumerate(blocks):
        up_val, up_gate = pending
        if j + 1 < len(blocks):
            pending = (up_block(blocks[j + 1]), up_block(D_FF + blocks[j + 1]))
        val = conv_block(up_val, c0)
        gate = conv_block(up_gate, D_FF + c0)
        acts.append((val * (gate * jax.nn.sigmoid(gate))).astype(BF16))
        if len(acts) == FFN_DOWN_GROUP or j + 1 == len(blocks):
            r0 = c0 + FFN_COL_BLOCK - len(acts) * FFN_COL_BLOCK
            part = jnp.dot(jnp.concatenate(acts, axis=1), down_ref[r0:c0 + FFN_COL_BLOCK, :],
                           preferred_element_type=F32)
            acc = part if acc is None else acc + part
            acts = []
    y = x + mod_ref[5:6, :] * acc
    if final:
        y = y * lax.rsqrt(jnp.mean(y * y, axis=-1, keepdims=True) + NORM_EPS) * fin_ref[...]
    xo_ref[...] = y


def _ffn(tl, final, x, mod, lp, tail0, norm_final):
    bsz, seq, _ = x.shape
    tf = tl.t_ffn
    consts = [lp["norm_ffn"], lp["ffn_up"], lp["ffn_conv_w"], lp["ffn_conv_b"], lp["ffn_down"]]
    x_spec = pl.BlockSpec((None, tf, D_MODEL), lambda b, i: (b, i, 0))
    in_specs = ([x_spec, _batch_spec(mod)] + [_const_spec(a) for a in consts]
                + [_batch_spec(tail0), _const_spec(norm_final)])
    x_new, tail = pl.pallas_call(
        functools.partial(_ffn_kernel, tl, final),
        grid=(bsz, seq // tf),
        in_specs=in_specs,
        out_specs=[x_spec, _batch_spec(tail0)],
        out_shape=[jax.ShapeDtypeStruct(x.shape, F32), jax.ShapeDtypeStruct(tail0.shape, F32)],
        compiler_params=pltpu.CompilerParams(
            dimension_semantics=("arbitrary", "arbitrary"), vmem_limit_bytes=VMEM_LIMIT_BYTES),
        name="conv_ffn",
    )(x, mod, *consts, tail0, norm_final)
    return x_new, tail


def _block_diag_dense(blocks):
    n, r, c = blocks.shape
    eye = jnp.eye(n, dtype=blocks.dtype)
    return (eye[:, None, :, None] * blocks[:, :, None, :]).reshape(n * r, n * c)


def _layer_params(p, l):
    row = lambda a: a.reshape(1, -1)
    zeros_lora = jnp.zeros((64, RWKV_WIDTH), F32)
    head = jnp.arange(RWKV_WIDTH) // RWKV_HEAD
    lp = {
        "norm_mix": row(p["norm_mix"][l]), "norm_ffn": row(p["norm_ffn"][l]),
        "w_in": p["w_in"][l].astype(BF16), "w_out": p["w_out"][l].astype(BF16),
        "lru_conv_w": p["lru_conv_w"][l], "lru_conv_b": row(p["lru_conv_b"][l]),
        "lru_wa": _block_diag_dense(p["lru_wa"][l]).astype(BF16), "lru_ba": row(p["lru_ba"][l]),
        "lru_wx": _block_diag_dense(p["lru_wx"][l]).astype(BF16), "lru_bx": row(p["lru_bx"][l]),
        "lru_lambda": row(p["lru_lambda"][l]),
        "rwkv_mu": row(p["rwkv_mu"][l]), "rwkv_w0": row(p["rwkv_w0"][l]),
        "rwkv_w2": jnp.concatenate([p["rwkv_w2"][l], zeros_lora], axis=0).astype(BF16),
        "rwkv_a0": row(p["rwkv_a0"][l]),
        "rwkv_a2": jnp.concatenate([zeros_lora, p["rwkv_a2"][l]], axis=0).astype(BF16),
        "rwkv_g2": p["rwkv_g2"][l].astype(BF16),
        "rwkv_k_k": row(p["rwkv_k_k"][l]), "rwkv_k_a": row(p["rwkv_k_a"][l]),
        "rwkv_r_k": row(p["rwkv_r_k"][l]), "rwkv_ln_w": row(p["rwkv_ln_w"][l]),
        "rwkv_ln_b": row(p["rwkv_ln_b"][l]),
        "ones_bd": (head[:256, None] == head[None, :256]).astype(BF16),
        "s5_a_re": row(p["s5_a_re"][l]), "s5_a_im": row(p["s5_a_im"][l]),
        "s5_log_dt": row(jnp.repeat(p["s5_log_dt"][l], 64)),
        "s5_b_re": _block_diag_dense(jnp.swapaxes(p["s5_b_re"][l], 1, 2)),
        "s5_b_im": _block_diag_dense(jnp.swapaxes(p["s5_b_im"][l], 1, 2)),
        "s5_c_re_f32": _block_diag_dense(jnp.swapaxes(p["s5_c_re"][l], 1, 2)),
        "s5_c_im_f32": _block_diag_dense(jnp.swapaxes(p["s5_c_im"][l], 1, 2)),
        "s5_d": row(p["s5_d"][l]),
        "s5_glu_w": p["s5_glu_w"][l].astype(BF16), "s5_glu_b": row(p["s5_glu_b"][l]),
        "ffn_up": p["ffn_up"][l].astype(BF16), "ffn_conv_w": p["ffn_conv_w"][l],
        "ffn_conv_b": row(p["ffn_conv_b"][l]), "ffn_down": p["ffn_down"][l].astype(BF16),
    }
    lp["s5_tables"] = _s5_prep(lp)
    lp["s5_c_re"] = lp["s5_c_re_f32"].astype(BF16)
    lp["s5_c_im"] = lp["s5_c_im_f32"].astype(BF16)
    return lp


def _pad_tail(buf):
    return jnp.pad(buf, ((0, 0), (SUBLANES - buf.shape[1], 0), (0, 0)))


def _state_in(states, l):
    lru_buf, lru_h, rw_shift, rw_s, s5_re, s5_im, ffn_buf = (s[l] for s in states)
    bsz = lru_h.shape[0]
    return {
        "lru_buf": _pad_tail(lru_buf),
        "lru_h": lru_h.reshape(bsz, 1, LRU_WIDTH),
        "rwkv_shift": rw_shift.reshape(bsz, 1, RWKV_COLS),
        "rwkv_s": jnp.transpose(rw_s, (0, 2, 1, 3)).reshape(bsz, RWKV_HEAD, RWKV_WIDTH),
        "s5_re": s5_re.reshape(bsz, 1, S5_STATES),
        "s5_im": s5_im.reshape(bsz, 1, S5_STATES),
    }, _pad_tail(ffn_buf)


def _state_out(st, ffn_tail):
    bsz = st["lru_h"].shape[0]
    return (st["lru_buf"][:, SUBLANES - (LRU_CONV - 1):, :],
            st["lru_h"].reshape(bsz, LRU_WIDTH),
            st["rwkv_shift"].reshape(bsz, RWKV_COLS),
            jnp.transpose(st["rwkv_s"].reshape(bsz, RWKV_HEAD, RWKV_HEADS, RWKV_HEAD), (0, 2, 1, 3)),
            st["s5_re"].reshape(bsz, 16, 64),
            st["s5_im"].reshape(bsz, 16, 64),
            ffn_tail[:, SUBLANES - 2:, :])


def _trunk(x, mods, states, layer_params, norm_final):
    depth = len(layer_params)
    tl = _tiling(x.shape[1])
    new = []
    for l in range(depth):
        st, ffn_tail0 = _state_in(states, l)
        x, st = _mixer(tl, x, mods[l], layer_params[l], st)
        x, ffn_tail = _ffn(tl, l == depth - 1, x, mods[l], layer_params[l], ffn_tail0, norm_final)
        new.append(_state_out(st, ffn_tail))
    stacked = tuple(jnp.stack([n[j] for n in new], axis=0) for j in range(7))
    return x, stacked


def _zero_states(depth, bsz):
    return (jnp.zeros((depth, bsz, LRU_CONV - 1, LRU_WIDTH), F32),
            jnp.zeros((depth, bsz, LRU_WIDTH), F32),
            jnp.zeros((depth, bsz, RWKV_COLS), F32),
            jnp.zeros((depth, bsz, RWKV_HEADS, RWKV_HEAD, RWKV_HEAD), F32),
            jnp.zeros((depth, bsz, 16, 64), F32),
            jnp.zeros((depth, bsz, 16, 64), F32),
            jnp.zeros((depth, bsz, 2, 2 * D_FF), F32))


def kernel(x_prompt, x_sample, c_prompt, c_sample, state_lru_conv, state_lru_h, state_rwkv_shift, state_rwkv_S, state_s5_re, state_s5_im, state_ffn_conv, w_ada, b_ada, norm_mix, norm_ffn, w_in, w_out, lru_conv_w, lru_conv_b, lru_wa, lru_ba, lru_wx, lru_bx, lru_lambda, rwkv_mu, rwkv_w0, rwkv_w2, rwkv_a0, rwkv_a2, rwkv_g2, rwkv_k_k, rwkv_k_a, rwkv_r_k, rwkv_ln_w, rwkv_ln_b, s5_a_re, s5_a_im, s5_b_re, s5_b_im, s5_c_re, s5_c_im, s5_d, s5_log_dt, s5_glu_w, s5_glu_b, ffn_up, ffn_conv_w, ffn_conv_b, ffn_down, norm_final):
    p = dict(norm_mix=norm_mix, norm_ffn=norm_ffn, w_in=w_in, w_out=w_out,
             lru_conv_w=lru_conv_w, lru_conv_b=lru_conv_b, lru_wa=lru_wa, lru_ba=lru_ba,
             lru_wx=lru_wx, lru_bx=lru_bx, lru_lambda=lru_lambda,
             rwkv_mu=rwkv_mu, rwkv_w0=rwkv_w0, rwkv_w2=rwkv_w2, rwkv_a0=rwkv_a0, rwkv_a2=rwkv_a2,
             rwkv_g2=rwkv_g2, rwkv_k_k=rwkv_k_k, rwkv_k_a=rwkv_k_a,
             rwkv_r_k=rwkv_r_k.reshape(rwkv_r_k.shape[0], RWKV_WIDTH),
             rwkv_ln_w=rwkv_ln_w, rwkv_ln_b=rwkv_ln_b,
             s5_a_re=s5_a_re, s5_a_im=s5_a_im, s5_b_re=s5_b_re, s5_b_im=s5_b_im,
             s5_c_re=s5_c_re, s5_c_im=s5_c_im, s5_d=s5_d, s5_log_dt=s5_log_dt,
             s5_glu_w=s5_glu_w, s5_glu_b=s5_glu_b,
             ffn_up=ffn_up, ffn_conv_w=ffn_conv_w, ffn_conv_b=ffn_conv_b, ffn_down=ffn_down)
    depth = w_in.shape[0]
    layer_params = [_layer_params(p, l) for l in range(depth)]
    fin = norm_final.reshape(1, D_MODEL)
    n_prompt = c_prompt.shape[0]
    mod = _ada(jnp.concatenate([c_prompt, c_sample], axis=0), w_ada, b_ada)
    mod = mod.reshape(depth, mod.shape[1], 6, D_MODEL)
    mods_p = [mod[l, :n_prompt] for l in range(depth)]
    mods_s = [mod[l, n_prompt:] for l in range(depth)]
    s_in = (state_lru_conv, state_lru_h, state_rwkv_shift, state_rwkv_S, state_s5_re, state_s5_im,
            state_ffn_conv)
    y_sample, s_states = _trunk(x_sample, mods_s, s_in, layer_params, fin)
    y_prompt, p_states = _trunk(x_prompt, mods_p, _zero_states(depth, x_prompt.shape[0]), layer_params, fin)
    return (y_prompt, y_sample) + tuple(p_states) + tuple(s_states)
```

```python
import functools
import math
from typing import NamedTuple

import jax
import jax.numpy as jnp
from jax import lax
from jax.experimental import pallas as pl
from jax.experimental.pallas import tpu as pltpu

F32 = jnp.float32
BF16 = jnp.bfloat16

D_MODEL = 1024
LRU_WIDTH = 256
LRU_CONV = 4
LRU_C = 8.0
RWKV_WIDTH = 512
RWKV_HEAD = 64
RWKV_HEADS = 8
RWKV_COLS = 1792
S5_WIDTH = 256
S5_STATES = 1024
S5_POW_ROWS = 8
S5_SUB = 8
D_FF = 2816
NORM_EPS = 1e-6
RWKV_GN_EPS = 64e-5
SUBLANES = 8
FFN_COL_BLOCK = 256
FFN_DOWN_GROUP = 4
VMEM_LIMIT_BYTES = 56 * 1024 * 1024
PASSES_INTRA = 1
PASSES_INV = 3
INV_ACCURATE_ROUNDS = 3
PASSES_STATE = 1
PASSES_OUT = 1


class Tiling(NamedTuple):
    t_mix: int
    chunk: int
    group: int
    t_ffn: int


def _tiling(seq_len):
    t_mix = min(seq_len, 256)
    chunk = min(t_mix, 64)
    group = min(RWKV_HEADS, 256 // chunk)
    return Tiling(t_mix, chunk, group, min(seq_len, 512))


def _dg(a, b, ca, cb):
    return lax.dot_general(a, b, (((ca,), (cb,)), ((), ())), preferred_element_type=F32)


def _mm(a, w):
    return jnp.dot(a.astype(BF16), w, preferred_element_type=F32)


def _expm1(x):
    u = jnp.exp(x)
    small = u >= 0.5
    stable = (u - 1.0) * x / jnp.log(jnp.where(small, u, 0.75))
    return jnp.where(u == 1.0, x, jnp.where(small, stable, u - 1.0))


def _shift_rows(x, d, fill):
    if d % SUBLANES == 0:
        return jnp.concatenate([jnp.full((d, x.shape[1]), fill, x.dtype), x[:x.shape[0] - d]], axis=0)
    rolled = pltpu.roll(x, d, axis=0)
    if x.shape[0] <= SUBLANES:
        return jnp.where(lax.broadcasted_iota(jnp.int32, x.shape, 0) >= d, rolled, fill)
    r8 = lax.broadcasted_iota(jnp.int32, (SUBLANES, 1), 0)
    return jnp.concatenate([jnp.where(r8 >= d, rolled[0:SUBLANES], fill), rolled[SUBLANES:]], axis=0)


def _shift_in(x, d, tail):
    rolled = pltpu.roll(x, d, axis=0)
    head = rolled[0:SUBLANES]
    r8 = lax.broadcasted_iota(jnp.int32, (SUBLANES, 1), 0)
    for j in range(d):
        head = jnp.where(r8 == j, tail[SUBLANES - d + j:SUBLANES - d + j + 1], head)
    return jnp.concatenate([head, rolled[SUBLANES:]], axis=0)


def _linear_scan(a, b):
    d = 1
    while d < a.shape[0]:
        b = a * _shift_rows(b, d, 0.0) + b
        a = a * _shift_rows(a, d, 1.0)
        d *= 2
    return b


def _block_mask(rows, cols, rb, cb, dtype):
    r = lax.broadcasted_iota(jnp.int32, (rows, cols), 0) >> int(math.log2(rb))
    c = lax.broadcasted_iota(jnp.int32, (rows, cols), 1) >> int(math.log2(cb))
    return jnp.where(r == c, 1.0, 0.0).astype(dtype)


def _split(x):
    hi = x.astype(BF16)
    return hi, (x - hi.astype(F32)).astype(BF16)


def _lhs3(a, axis, passes=3):
    if passes == 1:
        return a.astype(BF16)
    hi, lo = _split(a)
    return jnp.concatenate([hi, hi, lo], axis=axis)


def _rhs3(b, axis, group=1, passes=3):
    parts = (b.astype(BF16),) if passes == 1 else _split(b)
    if group > 1:
        rows, cols = b.shape
        mask = _block_mask(group * rows, cols, rows, cols // group, BF16)
        parts = tuple(jnp.concatenate([q] * group, axis=0) * mask for q in parts)
    if passes == 1:
        return parts[0]
    return jnp.concatenate([parts[0], parts[1], parts[0]], axis=axis)


def _fold_diag(m, group):
    rows = m.shape[0] // group
    width = m.shape[1] // group
    rb = lax.broadcasted_iota(jnp.int32, m.shape, 0) >> int(math.log2(rows))
    cb = lax.broadcasted_iota(jnp.int32, m.shape, 1) >> int(math.log2(width))
    m = jnp.where(rb == cb, m, 0.0)
    out = m[0:rows]
    for h in range(1, group):
        out = out + m[h * rows:(h + 1) * rows]
    return out


def _rmsnorm_mod(x, gain, scale, shift):
    y = x * lax.rsqrt(jnp.mean(x * x, axis=-1, keepdims=True) + NORM_EPS)
    return (y * gain) * (1.0 + scale) + shift


def _ada_kernel(c_ref, w_ref, b_ref, o_ref):
    c = c_ref[...]
    o_ref[...] = _mm(c * jax.nn.sigmoid(c), w_ref[...].astype(BF16)) + b_ref[...]


def _ada(c, w_ada, b_ada):
    depth, _, n = w_ada.shape
    rows = c.shape[0]
    tn = 1536
    return pl.pallas_call(
        _ada_kernel,
        grid=(depth, n // tn),
        in_specs=[pl.BlockSpec((rows, D_MODEL), lambda l, j: (0, 0)),
                  pl.BlockSpec((None, D_MODEL, tn), lambda l, j: (l, 0, j)),
                  pl.BlockSpec((None, 1, tn), lambda l, j: (l, 0, j))],
        out_specs=pl.BlockSpec((None, rows, tn), lambda l, j: (l, 0, j)),
        out_shape=jax.ShapeDtypeStruct((depth, rows, n), F32),
        compiler_params=pltpu.CompilerParams(
            dimension_semantics=("arbitrary", "arbitrary"), vmem_limit_bytes=VMEM_LIMIT_BYTES),
        name="ada_mod",
    )(c, w_ada, b_ada.reshape(depth, 1, n))


def _s5_prep_kernel(are_ref, aim_ref, ldt_ref, bre_ref, bim_ref, cre_ref, cim_ref,
                    bbar_re, bbar_im, pw_re, pw_im, lagk, tab_end_re, tab_end_im, tab_in_re, tab_in_im):
    dt = jnp.exp(ldt_ref[...])
    a_re = are_ref[...]
    a_im = aim_ref[...]
    mag = jnp.exp(dt * a_re)
    abr = mag * jnp.cos(dt * a_im)
    abi = mag * jnp.sin(dt * a_im)
    den = a_re * a_re + a_im * a_im
    fr = ((abr - 1.0) * a_re + abi * a_im) / den
    fi = (abi * a_re - (abr - 1.0) * a_im) / den
    bb_re = fr * bre_ref[...] - fi * bim_ref[...]
    bb_im = fr * bim_ref[...] + fi * bre_ref[...]
    bbar_re[...] = bb_re.astype(BF16)
    bbar_im[...] = bb_im.astype(BF16)
    pr, pi = abr, abi
    for k in range(S5_POW_ROWS):
        pw_re[k:k + 1, :] = pr
        pw_im[k:k + 1, :] = pi
        pr, pi = pr * pr - pi * pi, 2.0 * pr * pi
    c3_re = _rhs3(cre_ref[...], 0)
    c3_im = _rhs3(cim_ref[...], 0)
    pr, pi = jnp.ones_like(abr), jnp.zeros_like(abi)
    for d in range(S5_SUB + 1):
        if d < S5_SUB:
            lagk[d * S5_WIDTH:(d + 1) * S5_WIDTH, :] = (
                _dg(_lhs3(bb_re * pr - bb_im * pi, 1), c3_re, 1, 0)
                - _dg(_lhs3(bb_re * pi + bb_im * pr, 1), c3_im, 1, 0)).astype(BF16)
            tab_end_re[S5_SUB - 1 - d:S5_SUB - d, :] = pr
            tab_end_im[S5_SUB - 1 - d:S5_SUB - d, :] = pi
        if d > 0:
            tab_in_re[d - 1:d, :] = pr
            tab_in_im[d - 1:d, :] = pi
        pr, pi = pr * abr - pi * abi, pr * abi + pi * abr


def _s5_prep(lp):
    ins = [lp["s5_a_re"], lp["s5_a_im"], lp["s5_log_dt"], lp["s5_b_re"], lp["s5_b_im"],
           lp["s5_c_re_f32"], lp["s5_c_im_f32"]]
    return pl.pallas_call(
        _s5_prep_kernel,
        out_shape=[jax.ShapeDtypeStruct((S5_WIDTH, S5_STATES), BF16)] * 2
        + [jax.ShapeDtypeStruct((S5_POW_ROWS, S5_STATES), F32)] * 2
        + [jax.ShapeDtypeStruct((S5_SUB * S5_WIDTH, S5_WIDTH), BF16)]
        + [jax.ShapeDtypeStruct((S5_SUB, S5_STATES), F32)] * 4,
        compiler_params=pltpu.CompilerParams(vmem_limit_bytes=VMEM_LIMIT_BYTES),
        name="s5_prep",
    )(*ins)


def _mixer_kernel(tl,
                  x_ref, mod_ref, norm_ref, w_in_ref, w_out_ref,
                  conv_w_ref, conv_b_ref, wa_ref, ba_ref, wx_ref, bx_ref, lam_ref,
                  lru_buf0_ref, lru_h0_ref,
                  mu_ref, w0_ref, w2_ref, a0_ref, a2_ref, g2_ref, kk_ref, ka_ref, rk_ref,
                  lnw_ref, lnb_ref, ones_ref, shift0_ref, s0_ref,
                  bbar_re, bbar_im, pw_re, pw_im, lagk_ref, tab_end_re, tab_end_im, tab_in_re, tab_in_im,
                  lag_shift_ref, sub_sum_ref, cre_ref, cim_ref, d_ref,
                  gluw_ref, glub_ref, s5re0_ref, s5im0_ref,
                  xo_ref, lru_buf_o, lru_h_o, shift_o, s_o, s5re_o, s5im_o):
    tc, chunk, group = tl.t_mix, tl.chunk, tl.group
    assert int(math.log2(S5_SUB)) + int(math.log2(max(tc // S5_SUB, 2))) <= S5_POW_ROWS
    i = pl.program_id(1)

    @pl.when(i == 0)
    def _init():
        lru_buf_o[...] = lru_buf0_ref[...]
        lru_h_o[...] = lru_h0_ref[...]
        shift_o[...] = shift0_ref[...]
        s_o[...] = s0_ref[...]
        s5re_o[...] = s5re0_ref[...]
        s5im_o[...] = s5im0_ref[...]

    x = x_ref[...]
    r8 = lax.broadcasted_iota(jnp.int32, (SUBLANES, 1), 0)
    h = _rmsnorm_mod(x, norm_ref[...], mod_ref[1:2, :], mod_ref[0:1, :])
    proj = _mm(h, w_in_ref[...])
    u_gate = proj[:, 0:256]
    u_lru = proj[:, 256:512]
    p_rw = proj[:, 512:2304]
    u_s5 = proj[:, 2304:2560]

    branch = {}

    def lru_step():
        cw = conv_w_ref[...]
        lru_tail = lru_buf_o[...]
        xc = conv_b_ref[...] + _shift_in(u_lru, LRU_CONV - 1, lru_tail) * cw[0:1]
        for k in range(1, LRU_CONV - 1):
            xc = xc + _shift_in(u_lru, LRU_CONV - 1 - k, lru_tail) * cw[k:k + 1]
        xc = xc + u_lru * cw[LRU_CONV - 1:LRU_CONV]
        lru_buf_o[...] = u_lru[tc - SUBLANES:tc, :]
        r_gate = jax.nn.sigmoid(_mm(xc, wa_ref[...]) + ba_ref[...])
        i_gate = jax.nn.sigmoid(_mm(xc, wx_ref[...]) + bx_ref[...])
        log_a = -LRU_C * r_gate * jax.nn.softplus(-lam_ref[...])
        a = jnp.exp(log_a)
        gain = jnp.sqrt(jnp.maximum(-_expm1(2.0 * log_a), 0.0))
        b = gain * i_gate * xc
        b = jnp.concatenate([b[0:SUBLANES] + jnp.where(r8 == 0, a[0:SUBLANES] * lru_h_o[...], 0.0),
                             b[SUBLANES:]], axis=0)
        h_lru = _linear_scan(a, b)
        lru_h_o[...] = h_lru[tc - 1:tc, :]
        branch["y_a"] = h_lru * jax.nn.gelu(u_gate)

    n_sub = tc // S5_SUB
    u_bf = u_s5.astype(BF16)

    def s5_lag_step():
        lag_shift = lag_shift_ref[...]
        lb = lag_shift.shape[1]
        rows = []
        for r0 in range(0, tc, lb):
            u_blk = u_bf[r0:r0 + lb]
            lagged = jnp.dot(lag_shift, u_blk, preferred_element_type=F32)
            rows.append(jnp.concatenate(
                [u_blk] + [lagged[(d - 1) * lb:d * lb].astype(BF16) for d in range(1, S5_SUB)], axis=1))
        branch["lags"] = jnp.concatenate(rows, axis=0)

    def s5_intra_step():
        branch["y_intra"] = jnp.dot(branch["lags"], lagk_ref[...], preferred_element_type=F32)

    def s5_input_step():
        branch["bu"] = (_mm(u_bf, bbar_re[...]), _mm(u_bf, bbar_im[...]))

    def s5_state_step():
        bu_re, bu_im = branch["bu"]
        te_re = jnp.concatenate([tab_end_re[...]] * n_sub, axis=0)
        te_im = jnp.concatenate([tab_end_im[...]] * n_sub, axis=0)
        sub_sum = sub_sum_ref[...]
        e_re = _mm(sub_sum, (te_re * bu_re - te_im * bu_im).astype(BF16))[0:n_sub]
        e_im = _mm(sub_sum, (te_re * bu_im + te_im * bu_re).astype(BF16))[0:n_sub]
        h0r = s5re_o[...]
        h0i = s5im_o[...]
        k_sub = int(math.log2(S5_SUB))
        ar = pw_re[k_sub:k_sub + 1, :]
        ai = pw_im[k_sub:k_sub + 1, :]
        if n_sub <= SUBLANES:
            cr, ci = h0r, h0i
            prev_r, prev_i = [], []
            for n in range(n_sub):
                prev_r.append(jnp.broadcast_to(cr, (S5_SUB, S5_STATES)))
                prev_i.append(jnp.broadcast_to(ci, (S5_SUB, S5_STATES)))
                cr, ci = ar * cr - ai * ci + e_re[n:n + 1], ar * ci + ai * cr + e_im[n:n + 1]
            s5re_o[...] = cr
            s5im_o[...] = ci
            pvr = jnp.concatenate(prev_r, axis=0)
            pvi = jnp.concatenate(prev_i, axis=0)
        else:
            rn = lax.broadcasted_iota(jnp.int32, (n_sub, 1), 0)
            hr = e_re + jnp.where(rn == 0, ar * h0r - ai * h0i, 0.0)
            hi = e_im + jnp.where(rn == 0, ar * h0i + ai * h0r, 0.0)
            d = 1
            while d < n_sub:
                k = k_sub + int(math.log2(d))
                pr = pw_re[k:k + 1, :]
                pi = pw_im[k:k + 1, :]
                sr = _shift_rows(hr, d, 0.0)
                si = _shift_rows(hi, d, 0.0)
                hr, hi = hr + (pr * sr - pi * si), hi + (pr * si + pi * sr)
                d *= 2
            s5re_o[...] = hr[n_sub - 1:n_sub, :]
            s5im_o[...] = hi[n_sub - 1:n_sub, :]
            pvr = jnp.where(rn == 0, h0r, pltpu.roll(hr, 1, axis=0))
            pvi = jnp.where(rn == 0, h0i, pltpu.roll(hi, 1, axis=0))
            pvr = jnp.broadcast_to(pvr[:, None, :], (n_sub, S5_SUB, S5_STATES)).reshape(tc, S5_STATES)
            pvi = jnp.broadcast_to(pvi[:, None, :], (n_sub, S5_SUB, S5_STATES)).reshape(tc, S5_STATES)
        branch["pv"] = (pvr, pvi)

    def s5_out_step():
        pvr, pvi = branch["pv"]
        ti_re = jnp.concatenate([tab_in_re[...]] * n_sub, axis=0)
        ti_im = jnp.concatenate([tab_in_im[...]] * n_sub, axis=0)
        y_inter = (_mm(ti_re * pvr - ti_im * pvi, cre_ref[...])
                   - _mm(ti_re * pvi + ti_im * pvr, cim_ref[...]))
        branch["y5"] = branch["y_intra"] + y_inter + d_ref[...] * u_s5

    def s5_glu_step():
        z = jax.nn.gelu(branch["y5"])
        branch["y_c"] = z * jax.nn.sigmoid(_mm(z, gluw_ref[...]) + glub_ref[...])

    fillers = iter([s5_lag_step, lru_step, s5_intra_step, s5_input_step, s5_state_step, s5_out_step,
                    s5_glu_step])

    def fill():
        step = next(fillers, None)
        if step is not None:
            step()

    rolled = pltpu.roll(p_rw, 1, axis=0)
    prev =jnp.concatenate([jnp.where(r8 == 0, shift_o[...], rolled[0:SUBLANES]), rolled[SUBLANES:]], axis=0)
    shift_o[...] = p_rw[tc - 1:tc, :]
    ones_bd = ones_ref[...]

    def head_sum(z):
        w = ones_bd.shape[0]
        return jnp.concatenate([_mm(z[:, j:j + w], ones_bd) for j in range(0, RWKV_WIDTH, w)], axis=1)

    def rwkv_operands(rows):
        xm = p_rw[rows] + (prev[rows] - p_rw[rows]) * mu_ref[...]
        r_t = xm[:, 0:512]
        k_t = xm[:, 512:1024]
        v_t = xm[:, 1024:1536]
        lo = xm[:, 1536:1664]
        g_lo = xm[:, 1664:1792]
        log_w = -jax.nn.softplus(-(w0_ref[...] + _mm(jnp.tanh(lo), w2_ref[...]))) - 0.5
        a_t = jax.nn.sigmoid(a0_ref[...] + _mm(lo, a2_ref[...]))
        g_t = _mm(jax.nn.sigmoid(g_lo), g2_ref[...])
        kk = k_t * kk_ref[...]
        kk = kk * lax.rsqrt(jnp.maximum(head_sum(kk * kk), 1e-24))
        kp = k_t * (1.0 + (a_t - 1.0) * ka_ref[...])
        ld_t = -jnp.exp(log_w)
        return r_t, kp, v_t, kk, kk * a_t, ld_t, g_t, r_t * kp * rk_ref[...]

    gw = group * RWKV_HEAD
    crow = lax.broadcasted_iota(jnp.int32, (chunk, group * chunk), 0)
    ccol = lax.broadcasted_iota(jnp.int32, (chunk, group * chunk), 1) & (chunk - 1)
    strict = ccol < crow
    incl = ccol <= crow
    diag = ccol == crow
    n_pow = int(math.log2(chunk))

    n_chunks = tc // chunk
    n_groups = RWKV_HEADS // group
    lanes = [slice(g * gw, (g + 1) * gw) for g in range(n_groups)]

    a3s, kgbg3, v_c, g_end_c, lv, rb3, t3, gate_c, rkr_c, vrow_c = {}, {}, {}, {}, {}, {}, {}, {}, {}, {}

    def prepare_chunks(c0, c1):
        ops = rwkv_operands(slice(c0 * chunk, c1 * chunk))
        chains = [(c, g) for c in range(c0, c1) for g in range(n_groups)]
        a3, bb3, kb3, v3 = {}, {}, {}, {}
        for c in range(c0, c1):
            sl = slice((c - c0) * chunk, (c - c0 + 1) * chunk)
            r, kpc, v, kkc, bq, ld, gate_c[c], rkr_c[c] = (z[sl] for z in ops)
            vrow_c[c] = v
            _stage_chunk(c, r, kpc, v, kkc, bq, ld, a3, bb3, kb3, v3)
        _solve_chunks(chains, a3, bb3, kb3, v3)

    def _stage_chunk(c, r, kpc, v, kkc, bq, ld, a3, bb3, kb3, v3):
        cum = ld
        d = 1
        while d < chunk:
            cum = cum + _shift_rows(cum, d, 0.0)
            d *= 2
        cum_end = cum[chunk - 1:chunk, :]
        inv = jnp.exp(-cum)
        tail_decay = jnp.exp(cum_end - cum)
        rt = r * jnp.exp(cum)
        kkt = kkc * jnp.exp(cum - ld)
        kb = kpc * inv
        bb = bq * inv
        kg = kpc * tail_decay
        bg = bq * tail_decay
        g_end_c[c] = jnp.exp(cum_end)
        for g, ls in enumerate(lanes):
            a_in = jnp.concatenate([kkt[:, ls], rt[:, ls]], axis=0)
            a3[c, g] = _lhs3(a_in, 1, PASSES_INTRA)
            a3s[c, g] = a3[c, g] if PASSES_STATE == PASSES_INTRA else _lhs3(a_in, 1, PASSES_STATE)
            bb3[c, g] = _rhs3(bb[:, ls], 1, group, PASSES_INTRA)
            kb3[c, g] = _rhs3(kb[:, ls], 1, group, PASSES_INTRA)
            v3[c, g] = _rhs3(v[:, ls], 0, group, PASSES_INTRA)
            kgbg3[c, g] = _rhs3(jnp.concatenate([kg[:, ls], -bg[:, ls]], axis=0), 0, 1, PASSES_STATE)
            v_c[c, g] = v[:, ls]

    def _solve_chunks(chains, a3, bb3, kb3, v3):
        xb = {ch: _dg(a3[ch], bb3[ch], 1, 1) for ch in chains}
        xk = {ch: _dg(a3[ch], kb3[ch], 1, 1) for ch in chains}
        n = {ch: jnp.where(strict, -xb[ch][0:chunk], 0.0) for ch in chains}
        for ch in chains:
            rb3[ch] = _lhs3(jnp.where(incl, xb[ch][chunk:], 0.0), 1, PASSES_OUT)
            lv[ch] = _dg(_lhs3(jnp.concatenate([jnp.where(strict, xk[ch][0:chunk], 0.0),
                                                 jnp.where(incl, xk[ch][chunk:], 0.0)], axis=0), 1, PASSES_INTRA),
                         v3[ch], 1, 0)
        t = {ch: jnp.where(diag, 1.0, n[ch]) for ch in chains}
        p = {ch: _dg(_lhs3(n[ch], 1, PASSES_INV), _rhs3(n[ch], 0, group, PASSES_INV), 1, 0) for ch in chains}
        for q in range(1, n_pow):
            last = q + 1 == n_pow
            passes = PASSES_INV if q <= INV_ACCURATE_ROUNDS else 1
            tp = {ch: _dg(_lhs3(t[ch] if last else jnp.concatenate([t[ch], p[ch]], axis=0), 1, passes),
                          _rhs3(p[ch], 0, group, passes), 1, 0) for ch in chains}
            t = {ch: t[ch] + tp[ch][0:chunk] for ch in chains}
            if not last:
                p = {ch: tp[ch][chunk:] for ch in chains}
        for ch in chains:
            t3[ch] = _lhs3(t[ch], 1, PASSES_STATE)

    prepare_chunks(0, n_chunks)

    inv_n = 1.0 / RWKV_HEAD

    def chunk_out(c, y):
        yc = y - head_sum(y) * inv_n
        var = head_sum(yc * yc) * inv_n
        yn = yc * lax.rsqrt(var + RWKV_GN_EPS) * lnw_ref[...] + lnb_ref[...]
        return (yn + head_sum(rkr_c[c]) * vrow_c[c]) * gate_c[c]

    yb_rows = []
    y_prev = None
    for c in range(n_chunks):
        s_prev = [s_o[:, ls] for ls in lanes]
        xs = [_dg(a3s[c, g], _rhs3(s_prev[g], 1, group, PASSES_STATE), 1, 1) for g in range(n_groups)]
        if y_prev is not None:
            yb_rows.append(chunk_out(c - 1, y_prev))
        fill()
        u = [_dg(t3[c, g], _rhs3(xs[g][0:chunk] + lv[c, g][0:chunk], 0, group, PASSES_STATE), 1, 0)
             for g in range(n_groups)]
        fill()
        ru = [_dg(rb3[c, g], _rhs3(u[g], 0, group, PASSES_OUT), 1, 0) for g in range(n_groups)]
        upd = [_dg(_lhs3(jnp.concatenate([v_c[c, g], u[g]], axis=0), 0, PASSES_STATE), kgbg3[c, g], 0, 0)
               for g in range(n_groups)]
        y_prev = jnp.concatenate(
            [xs[g][chunk:] + lv[c, g][chunk:] - ru[g] for g in range(n_groups)], axis=1)
        for g, ls in enumerate(lanes):
            s_o[:, ls] = s_prev[g] * g_end_c[c][:, ls] + _fold_diag(upd[g], group)
    yb_rows.append(chunk_out(n_chunks - 1, y_prev))
    y_b = jnp.concatenate(yb_rows, axis=0)

    for _ in range(7):
        fill()
    mix = _mm(jnp.concatenate([branch["y_a"], y_b, branch["y_c"]], axis=-1), w_out_ref[...])
    xo_ref[...] = x + mod_ref[2:3, :] * mix


def _lag_shift_matrix(tc):
    t = jnp.arange(tc)
    blocks = [((t[:, None] - d == t[None, :]) & ((t % S5_SUB) >= d)[:, None]) for d in range(1, S5_SUB)]
    return jnp.concatenate(blocks, axis=0).astype(BF16)


def _const_spec(arr):
    nd = arr.ndim
    return pl.BlockSpec(arr.shape, lambda b, i: (0,) * nd)


def _batch_spec(arr):
    nd = arr.ndim
    return pl.BlockSpec((None,) + arr.shape[1:], lambda b, i: (b,) + (0,) * (nd - 1))


def _mixer(tl, x, mod, lp, st):
    bsz, seq, _ = x.shape
    tc = tl.t_mix
    consts1 = [lp["norm_mix"], lp["w_in"], lp["w_out"],
               lp["lru_conv_w"], lp["lru_conv_b"], lp["lru_wa"], lp["lru_ba"], lp["lru_wx"],
               lp["lru_bx"], lp["lru_lambda"]]
    states1 = [st["lru_buf"], st["lru_h"]]
    consts2 = [lp["rwkv_mu"], lp["rwkv_w0"], lp["rwkv_w2"], lp["rwkv_a0"], lp["rwkv_a2"],
               lp["rwkv_g2"], lp["rwkv_k_k"], lp["rwkv_k_a"], lp["rwkv_r_k"], lp["rwkv_ln_w"],
               lp["rwkv_ln_b"], lp["ones_bd"]]
    states2 = [st["rwkv_shift"], st["rwkv_s"]]
    sub_rows = max(tc // S5_SUB, SUBLANES)
    sub_sum = (jnp.arange(tc)[None, :] // S5_SUB == jnp.arange(sub_rows)[:, None]).astype(BF16)
    consts3 = list(lp["s5_tables"]) + [_lag_shift_matrix(min(tc, 256)), sub_sum, lp["s5_c_re"], lp["s5_c_im"],
                                       lp["s5_d"], lp["s5_glu_w"], lp["s5_glu_b"]]
    states3 = [st["s5_re"], st["s5_im"]]
    x_spec = pl.BlockSpec((None, tc, D_MODEL), lambda b, i: (b, i, 0))
    in_specs = ([x_spec, _batch_spec(mod)]
                + [_const_spec(a) for a in consts1] + [_batch_spec(a) for a in states1]
                + [_const_spec(a) for a in consts2] + [_batch_spec(a) for a in states2]
                + [_const_spec(a) for a in consts3] + [_batch_spec(a) for a in states3])
    state_list = states1 + states2 + states3
    out_shape = [jax.ShapeDtypeStruct(x.shape, F32)] + [jax.ShapeDtypeStruct(a.shape, F32) for a in state_list]
    out_specs = [x_spec] + [_batch_spec(a) for a in state_list]
    outs = pl.pallas_call(
        functools.partial(_mixer_kernel, tl),
        grid=(bsz, seq // tc),
        in_specs=in_specs, out_specs=out_specs, out_shape=out_shape,
        compiler_params=pltpu.CompilerParams(
            dimension_semantics=("arbitrary", "arbitrary"), vmem_limit_bytes=VMEM_LIMIT_BYTES),
        name="mixer",
    )(x, mod, *consts1, *states1, *consts2, *states2, *consts3, *states3)
    x_new = outs[0]
    keys = ["lru_buf", "lru_h", "rwkv_shift", "rwkv_s", "s5_re", "s5_im"]
    return x_new, dict(zip(keys, outs[1:]))


def _ffn_kernel(tl, final,
                x_ref, mod_ref, norm_ref, up_ref, cw_ref, cb_ref, down_ref, tail0_ref, fin_ref,
                xo_ref, tail_o):
    tf = tl.t_ffn
    i = pl.program_id(1)

    @pl.when(i == 0)
    def _init():
        tail_o[...] = tail0_ref[...]

    x = x_ref[...]
    h = _rmsnorm_mod(x, norm_ref[...], mod_ref[4:5, :], mod_ref[3:4, :]).astype(BF16)

    def up_block(c0):
        return jnp.dot(h, up_ref[:, c0:c0 + FFN_COL_BLOCK], preferred_element_type=F32)

    def conv_block(up, c0):
        cs = slice(c0, c0 + FFN_COL_BLOCK)
        tail = tail_o[:, cs]
        m1 = _shift_in(up, 1, tail)
        m2 = _shift_in(up, 2, tail)
        tail_o[:, cs] = up[tf - SUBLANES:tf, :]
        return cb_ref[:, cs] + m2 * cw_ref[0:1, cs] + m1 * cw_ref[1:2, cs] + up * cw_ref[2:3, cs]

    blocks = list(range(0, D_FF, FFN_COL_BLOCK))
    pending = (up_block(blocks[0]), up_block(D_FF + blocks[0]))
    acc = None
    acts = []
    for j, c0 in enumerate(blocks):
        up_val, up_gate = pending
        if j + 1 < len(blocks):
            pending = (up_block(blocks[j + 1]), up_block(D_FF + blocks[j + 1]))
        val = conv_block(up_val, c0)
        gate = conv_block(up_gate, D_FF + c0)
        acts.append((val * (gate * jax.nn.sigmoid(gate))).astype(BF16))
        if len(acts) == FFN_DOWN_GROUP or j + 1 == len(blocks):
            r0 = c0 + FFN_COL_BLOCK - len(acts) * FFN_COL_BLOCK
            part = jnp.dot(jnp.concatenate(acts, axis=1), down_ref[r0:c0 + FFN_COL_BLOCK, :],
                           preferred_element_type=F32)
            acc = part if acc is None else acc + part
            acts = []
    y = x + mod_ref[5:6, :] * acc
    if final:
        y = y * lax.rsqrt(jnp.mean(y * y, axis=-1, keepdims=True) + NORM_EPS) * fin_ref[...]
    xo_ref[...] = y


def _ffn(tl, final, x, mod, lp, tail0, norm_final):
    bsz, seq, _ = x.shape
    tf = tl.t_ffn
    consts = [lp["norm_ffn"], lp["ffn_up"], lp["ffn_conv_w"], lp["ffn_conv_b"], lp["ffn_down"]]
    x_spec = pl.BlockSpec((None, tf, D_MODEL), lambda b, i: (b, i, 0))
    in_specs = ([x_spec, _batch_spec(mod)] + [_const_spec(a) for a in consts]
                + [_batch_spec(tail0), _const_spec(norm_final)])
    x_new, tail = pl.pallas_call(
        functools.partial(_ffn_kernel, tl, final),
        grid=(bsz, seq // tf),
        in_specs=in_specs,
        out_specs=[x_spec, _batch_spec(tail0)],
        out_shape=[jax.ShapeDtypeStruct(x.shape, F32), jax.ShapeDtypeStruct(tail0.shape, F32)],
        compiler_params=pltpu.CompilerParams(
            dimension_semantics=("arbitrary", "arbitrary"), vmem_limit_bytes=VMEM_LIMIT_BYTES),
        name="conv_ffn",
    )(x, mod, *consts, tail0, norm_final)
    return x_new, tail


def _block_diag_dense(blocks):
    n, r, c = blocks.shape
    eye = jnp.eye(n, dtype=blocks.dtype)
    return (eye[:, None, :, None] * blocks[:, :, None, :]).reshape(n * r, n * c)


def _layer_params(p, l):
    row = lambda a: a.reshape(1, -1)
    zeros_lora = jnp.zeros((64, RWKV_WIDTH), F32)
    head = jnp.arange(RWKV_WIDTH) // RWKV_HEAD
    lp = {
        "norm_mix": row(p["norm_mix"][l]), "norm_ffn": row(p["norm_ffn"][l]),
        "w_in": p["w_in"][l].astype(BF16), "w_out": p["w_out"][l].astype(BF16),
        "lru_conv_w": p["lru_conv_w"][l], "lru_conv_b": row(p["lru_conv_b"][l]),
        "lru_wa": _block_diag_dense(p["lru_wa"][l]).astype(BF16), "lru_ba": row(p["lru_ba"][l]),
        "lru_wx": _block_diag_dense(p["lru_wx"][l]).astype(BF16), "lru_bx": row(p["lru_bx"][l]),
        "lru_lambda": row(p["lru_lambda"][l]),
        "rwkv_mu": row(p["rwkv_mu"][l]), "rwkv_w0": row(p["rwkv_w0"][l]),
        "rwkv_w2": jnp.concatenate([p["rwkv_w2"][l], zeros_lora], axis=0).astype(BF16),
        "rwkv_a0": row(p["rwkv_a0"][l]),
        "rwkv_a2": jnp.concatenate([zeros_lora, p["rwkv_a2"][l]], axis=0).astype(BF16),
        "rwkv_g2": p["rwkv_g2"][l].astype(BF16),
        "rwkv_k_k": row(p["rwkv_k_k"][l]), "rwkv_k_a": row(p["rwkv_k_a"][l]),
        "rwkv_r_k": row(p["rwkv_r_k"][l]), "rwkv_ln_w": row(p["rwkv_ln_w"][l]),
        "rwkv_ln_b": row(p["rwkv_ln_b"][l]),
        "ones_bd": (head[:256, None] == head[None, :256]).astype(BF16),
        "s5_a_re": row(p["s5_a_re"][l]), "s5_a_im": row(p["s5_a_im"][l]),
        "s5_log_dt": row(jnp.repeat(p["s5_log_dt"][l], 64)),
        "s5_b_re": _block_diag_dense(jnp.swapaxes(p["s5_b_re"][l], 1, 2)),
        "s5_b_im": _block_diag_dense(jnp.swapaxes(p["s5_b_im"][l], 1, 2)),
        "s5_c_re_f32": _block_diag_dense(jnp.swapaxes(p["s5_c_re"][l], 1, 2)),
        "s5_c_im_f32": _block_diag_dense(jnp.swapaxes(p["s5_c_im"][l], 1, 2)),
        "s5_d": row(p["s5_d"][l]),
        "s5_glu_w": p["s5_glu_w"][l].astype(BF16), "s5_glu_b": row(p["s5_glu_b"][l]),
        "ffn_up": p["ffn_up"][l].astype(BF16), "ffn_conv_w": p["ffn_conv_w"][l],
        "ffn_conv_b": row(p["ffn_conv_b"][l]), "ffn_down": p["ffn_down"][l].astype(BF16),
    }
    lp["s5_tables"] = _s5_prep(lp)
    lp["s5_c_re"] = lp["s5_c_re_f32"].astype(BF16)
    lp["s5_c_im"] = lp["s5_c_im_f32"].astype(BF16)
    return lp


def _pad_tail(buf):
    return jnp.pad(buf, ((0, 0), (SUBLANES - buf.shape[1], 0), (0, 0)))


def _state_in(states, l):
    lru_buf, lru_h, rw_shift, rw_s, s5_re, s5_im, ffn_buf = (s[l] for s in states)
    bsz = lru_h.shape[0]
    return {
        "lru_buf": _pad_tail(lru_buf),
        "lru_h": lru_h.reshape(bsz, 1, LRU_WIDTH),
        "rwkv_shift": rw_shift.reshape(bsz, 1, RWKV_COLS),
        "rwkv_s": jnp.transpose(rw_s, (0, 2, 1, 3)).reshape(bsz, RWKV_HEAD, RWKV_WIDTH),
        "s5_re": s5_re.reshape(bsz, 1, S5_STATES),
        "s5_im": s5_im.reshape(bsz, 1, S5_STATES),
    }, _pad_tail(ffn_buf)


def _state_out(st, ffn_tail):
    bsz = st["lru_h"].shape[0]
    return (st["lru_buf"][:, SUBLANES - (LRU_CONV - 1):, :],
            st["lru_h"].reshape(bsz, LRU_WIDTH),
            st["rwkv_shift"].reshape(bsz, RWKV_COLS),
            jnp.transpose(st["rwkv_s"].reshape(bsz, RWKV_HEAD, RWKV_HEADS, RWKV_HEAD), (0, 2, 1, 3)),
            st["s5_re"].reshape(bsz, 16, 64),
            st["s5_im"].reshape(bsz, 16, 64),
            ffn_tail[:, SUBLANES - 2:, :])


def _trunk(x, mods, states, layer_params, norm_final):
    depth = len(layer_params)
    tl = _tiling(x.shape[1])
    new = []
    for l in range(depth):
        st, ffn_tail0 = _state_in(states, l)
        x, st = _mixer(tl, x, mods[l], layer_params[l], st)
        x, ffn_tail = _ffn(tl, l == depth - 1, x, mods[l], layer_params[l], ffn_tail0, norm_final)
        new.append(_state_out(st, ffn_tail))
    stacked = tuple(jnp.stack([n[j] for n in new], axis=0) for j in range(7))
    return x, stacked


def _zero_states(depth, bsz):
    return (jnp.zeros((depth, bsz, LRU_CONV - 1, LRU_WIDTH), F32),
            jnp.zeros((depth, bsz, LRU_WIDTH), F32),
            jnp.zeros((depth, bsz, RWKV_COLS), F32),
            jnp.zeros((depth, bsz, RWKV_HEADS, RWKV_HEAD, RWKV_HEAD), F32),
            jnp.zeros((depth, bsz, 16, 64), F32),
            jnp.zeros((depth, bsz, 16, 64), F32),
            jnp.zeros((depth, bsz, 2, 2 * D_FF), F32))


def kernel(x_prompt, x_sample, c_prompt, c_sample, state_lru_conv, state_lru_h, state_rwkv_shift, state_rwkv_S, state_s5_re, state_s5_im, state_ffn_conv, w_ada, b_ada, norm_mix, norm_ffn, w_in, w_out, lru_conv_w, lru_conv_b, lru_wa, lru_ba, lru_wx, lru_bx, lru_lambda, rwkv_mu, rwkv_w0, rwkv_w2, rwkv_a0, rwkv_a2, rwkv_g2, rwkv_k_k, rwkv_k_a, rwkv_r_k, rwkv_ln_w, rwkv_ln_b, s5_a_re, s5_a_im, s5_b_re, s5_b_im, s5_c_re, s5_c_im, s5_d, s5_log_dt, s5_glu_w, s5_glu_b, ffn_up, ffn_conv_w, ffn_conv_b, ffn_down, norm_final):
    p = dict(norm_mix=norm_mix, norm_ffn=norm_ffn, w_in=w_in, w_out=w_out,
             lru_conv_w=lru_conv_w, lru_conv_b=lru_conv_b, lru_wa=lru_wa, lru_ba=lru_ba,
             lru_wx=lru_wx, lru_bx=lru_bx, lru_lambda=lru_lambda,
             rwkv_mu=rwkv_mu, rwkv_w0=rwkv_w0, rwkv_w2=rwkv_w2, rwkv_a0=rwkv_a0, rwkv_a2=rwkv_a2,
             rwkv_g2=rwkv_g2, rwkv_k_k=rwkv_k_k, rwkv_k_a=rwkv_k_a,
             rwkv_r_k=rwkv_r_k.reshape(rwkv_r_k.shape[0], RWKV_WIDTH),
             rwkv_ln_w=rwkv_ln_w, rwkv_ln_b=rwkv_ln_b,
             s5_a_re=s5_a_re, s5_a_im=s5_a_im, s5_b_re=s5_b_re, s5_b_im=s5_b_im,
             s5_c_re=s5_c_re, s5_c_im=s5_c_im, s5_d=s5_d, s5_log_dt=s5_log_dt,
             s5_glu_w=s5_glu_w, s5_glu_b=s5_glu_b,
             ffn_up=ffn_up, ffn_conv_w=ffn_conv_w, ffn_conv_b=ffn_conv_b, ffn_down=ffn_down)
    depth = w_in.shape[0]
    layer_params = [_layer_params(p, l) for l in range(depth)]
    fin = norm_final.reshape(1, D_MODEL)
    n_prompt = c_prompt.shape[0]
    mod = _ada(jnp.concatenate([c_prompt, c_sample], axis=0), w_ada, b_ada)
    mod = mod.reshape(depth, mod.shape[1], 6, D_MODEL)
    mods_p = [mod[l, :n_prompt] for l in range(depth)]
    mods_s = [mod[l, n_prompt:] for l in range(depth)]
    s_in = (state_lru_conv, state_lru_h, state_rwkv_shift, state_rwkv_S, state_s5_re, state_s5_im,
            state_ffn_conv)
    y_sample, s_states = _trunk(x_sample, mods_s, s_in, layer_params, fin)
    y_prompt, p_states = _trunk(x_prompt, mods_p, _zero_states(depth, x_prompt.shape[0]), layer_params, fin)
    return (y_prompt, y_sample) + tuple(p_states) + tuple(s_states)
```

```python
import functools
import math
from typing import NamedTuple

import jax
import jax.numpy as jnp
from jax import lax
from jax.experimental import pallas as pl
from jax.experimental.pallas import tpu as pltpu

F32 = jnp.float32
BF16 = jnp.bfloat16

D_MODEL = 1024
LRU_WIDTH = 256
LRU_CONV = 4
LRU_C = 8.0
RWKV_WIDTH = 512
RWKV_HEAD = 64
RWKV_HEADS = 8
RWKV_COLS = 1792
S5_WIDTH = 256
S5_STATES = 1024
S5_POW_ROWS = 8
S5_SUB = 8
D_FF = 2816
NORM_EPS = 1e-6
RWKV_GN_EPS = 64e-5
SUBLANES = 8
FFN_COL_BLOCK = 256
FFN_DOWN_GROUP = 4
FFN_LOOKAHEAD = 3
VMEM_LIMIT_BYTES = 56 * 1024 * 1024
PASSES_INTRA = 1
PASSES_INV = 3
INV_ACCURATE_ROUNDS = 3
PASSES_STATE = 1
PASSES_OUT = 1


class Tiling(NamedTuple):
    t_mix: int
    chunk: int
    group: int
    t_ffn: int


def _tiling(seq_len):
    t_mix = min(seq_len, 256)
    chunk = min(t_mix, 64)
    group = min(RWKV_HEADS, 256 // chunk)
    return Tiling(t_mix, chunk, group, min(seq_len, 512))


def _dg(a, b, ca, cb):
    return lax.dot_general(a, b, (((ca,), (cb,)), ((), ())), preferred_element_type=F32)


def _mm(a, w):
    return jnp.dot(a.astype(BF16), w, preferred_element_type=F32)


def _expm1(x):
    u = jnp.exp(x)
    small = u >= 0.5
    stable = (u - 1.0) * x / jnp.log(jnp.where(small, u, 0.75))
    return jnp.where(u == 1.0, x, jnp.where(small, stable, u - 1.0))


def _shift_rows(x, d, fill):
    if d % SUBLANES == 0:
        return jnp.concatenate([jnp.full((d, x.shape[1]), fill, x.dtype), x[:x.shape[0] - d]], axis=0)
    rolled = pltpu.roll(x, d, axis=0)
    if x.shape[0] <= SUBLANES:
        return jnp.where(lax.broadcasted_iota(jnp.int32, x.shape, 0) >= d, rolled, fill)
    r8 = lax.broadcasted_iota(jnp.int32, (SUBLANES, 1), 0)
    return jnp.concatenate([jnp.where(r8 >= d, rolled[0:SUBLANES], fill), rolled[SUBLANES:]], axis=0)


def _shift_in(x, d, tail):
    rolled = pltpu.roll(x, d, axis=0)
    head = rolled[0:SUBLANES]
    r8 = lax.broadcasted_iota(jnp.int32, (SUBLANES, 1), 0)
    for j in range(d):
        head = jnp.where(r8 == j, tail[SUBLANES - d + j:SUBLANES - d + j + 1], head)
    return jnp.concatenate([head, rolled[SUBLANES:]], axis=0)


def _linear_scan(a, b):
    d = 1
    while d < a.shape[0]:
        b = a * _shift_rows(b, d, 0.0) + b
        a = a * _shift_rows(a, d, 1.0)
        d *= 2
    return b


def _block_mask(rows, cols, rb, cb, dtype):
    r = lax.broadcasted_iota(jnp.int32, (rows, cols), 0) >> int(math.log2(rb))
    c = lax.broadcasted_iota(jnp.int32, (rows, cols), 1) >> int(math.log2(cb))
    return jnp.where(r == c, 1.0, 0.0).astype(dtype)


def _split(x):
    hi = x.astype(BF16)
    return hi, (x - hi.astype(F32)).astype(BF16)


def _lhs3(a, axis, passes=3):
    if passes == 1:
        return a.astype(BF16)
    hi, lo = _split(a)
    return jnp.concatenate([hi, hi, lo], axis=axis)


def _rhs3(b, axis, group=1, passes=3):
    parts = (b.astype(BF16),) if passes == 1 else _split(b)
    if group > 1:
        rows, cols = b.shape
        mask = _block_mask(group * rows, cols, rows, cols // group, BF16)
        parts = tuple(jnp.concatenate([q] * group, axis=0) * mask for q in parts)
    if passes == 1:
        return parts[0]
    return jnp.concatenate([parts[0], parts[1], parts[0]], axis=axis)


def _fold_diag(m, group):
    rows = m.shape[0] // group
    width = m.shape[1] // group
    rb = lax.broadcasted_iota(jnp.int32, m.shape, 0) >> int(math.log2(rows))
    cb = lax.broadcasted_iota(jnp.int32, m.shape, 1) >> int(math.log2(width))
    m = jnp.where(rb == cb, m, 0.0)
    out = m[0:rows]
    for h in range(1, group):
        out = out + m[h * rows:(h + 1) * rows]
    return out


def _rmsnorm_mod(x, gain, scale, shift):
    y = x * lax.rsqrt(jnp.mean(x * x, axis=-1, keepdims=True) + NORM_EPS)
    return (y * gain) * (1.0 + scale) + shift


def _ada_kernel(c_ref, w_ref, b_ref, o_ref):
    c = c_ref[...]
    o_ref[...] = _mm(c * jax.nn.sigmoid(c), w_ref[...].astype(BF16)) + b_ref[...]


def _ada(c, w_ada, b_ada):
    depth, _, n = w_ada.shape
    rows = c.shape[0]
    tn = 1536
    return pl.pallas_call(
        _ada_kernel,
        grid=(depth, n // tn),
        in_specs=[pl.BlockSpec((rows, D_MODEL), lambda l, j: (0, 0)),
                  pl.BlockSpec((None, D_MODEL, tn), lambda l, j: (l, 0, j)),
                  pl.BlockSpec((None, 1, tn), lambda l, j: (l, 0, j))],
        out_specs=pl.BlockSpec((None, rows, tn), lambda l, j: (l, 0, j)),
        out_shape=jax.ShapeDtypeStruct((depth, rows, n), F32),
        compiler_params=pltpu.CompilerParams(
            dimension_semantics=("arbitrary", "arbitrary"), vmem_limit_bytes=VMEM_LIMIT_BYTES),
        name="ada_mod",
    )(c, w_ada, b_ada.reshape(depth, 1, n))


def _s5_prep_kernel(are_ref, aim_ref, ldt_ref, bre_ref, bim_ref, cre_ref, cim_ref,
                    bbar_re, bbar_im, pw_re, pw_im, lagk, tab_end_re, tab_end_im, tab_in_re, tab_in_im):
    dt = jnp.exp(ldt_ref[...])
    a_re = are_ref[...]
    a_im = aim_ref[...]
    mag = jnp.exp(dt * a_re)
    abr = mag * jnp.cos(dt * a_im)
    abi = mag * jnp.sin(dt * a_im)
    den = a_re * a_re + a_im * a_im
    fr = ((abr - 1.0) * a_re + abi * a_im) / den
    fi = (abi * a_re - (abr - 1.0) * a_im) / den
    bb_re = fr * bre_ref[...] - fi * bim_ref[...]
    bb_im = fr * bim_ref[...] + fi * bre_ref[...]
    bbar_re[...] = bb_re.astype(BF16)
    bbar_im[...] = bb_im.astype(BF16)
    pr, pi = abr, abi
    for k in range(S5_POW_ROWS):
        pw_re[k:k + 1, :] = pr
        pw_im[k:k + 1, :] = pi
        pr, pi = pr * pr - pi * pi, 2.0 * pr * pi
    c3_re = _rhs3(cre_ref[...], 0)
    c3_im = _rhs3(cim_ref[...], 0)
    pr, pi = jnp.ones_like(abr), jnp.zeros_like(abi)
    for d in range(S5_SUB + 1):
        if d < S5_SUB:
            lagk[d * S5_WIDTH:(d + 1) * S5_WIDTH, :] = (
                _dg(_lhs3(bb_re * pr - bb_im * pi, 1), c3_re, 1, 0)
                - _dg(_lhs3(bb_re * pi + bb_im * pr, 1), c3_im, 1, 0)).astype(BF16)
            tab_end_re[S5_SUB - 1 - d:S5_SUB - d, :] = pr
            tab_end_im[S5_SUB - 1 - d:S5_SUB - d, :] = pi
        if d > 0:
            tab_in_re[d - 1:d, :] = pr
            tab_in_im[d - 1:d, :] = pi
        pr, pi = pr * abr - pi * abi, pr * abi + pi * abr


def _s5_prep(lp):
    ins = [lp["s5_a_re"], lp["s5_a_im"], lp["s5_log_dt"], lp["s5_b_re"], lp["s5_b_im"],
           lp["s5_c_re_f32"], lp["s5_c_im_f32"]]
    return pl.pallas_call(
        _s5_prep_kernel,
        out_shape=[jax.ShapeDtypeStruct((S5_WIDTH, S5_STATES), BF16)] * 2
        + [jax.ShapeDtypeStruct((S5_POW_ROWS, S5_STATES), F32)] * 2
        + [jax.ShapeDtypeStruct((S5_SUB * S5_WIDTH, S5_WIDTH), BF16)]
        + [jax.ShapeDtypeStruct((S5_SUB, S5_STATES), F32)] * 4,
        compiler_params=pltpu.CompilerParams(vmem_limit_bytes=VMEM_LIMIT_BYTES),
        name="s5_prep",
    )(*ins)


def _mixer_kernel(tl,
                  x_ref, mod_ref, norm_ref, w_in_ref, w_out_ref,
                  conv_w_ref, conv_b_ref, wa_ref, ba_ref, wx_ref, bx_ref, lam_ref,
                  lru_buf0_ref, lru_h0_ref,
                  mu_ref, w0_ref, w2_ref, a0_ref, a2_ref, g2_ref, kk_ref, ka_ref, rk_ref,
                  lnw_ref, lnb_ref, ones_ref, shift0_ref, s0_ref,
                  bbar_re, bbar_im, pw_re, pw_im, lagk_ref, tab_end_re, tab_end_im, tab_in_re, tab_in_im,
                  lag_shift_ref, sub_sum_ref, cre_ref, cim_ref, d_ref,
                  gluw_ref, glub_ref, s5re0_ref, s5im0_ref,
                  xo_ref, lru_buf_o, lru_h_o, shift_o, s_o, s5re_o, s5im_o):
    tc, chunk, group = tl.t_mix, tl.chunk, tl.group
    assert int(math.log2(S5_SUB)) + int(math.log2(max(tc // S5_SUB, 2))) <= S5_POW_ROWS
    i = pl.program_id(1)

    @pl.when(i == 0)
    def _init():
        lru_buf_o[...] = lru_buf0_ref[...]
        lru_h_o[...] = lru_h0_ref[...]
        shift_o[...] = shift0_ref[...]
        s_o[...] = s0_ref[...]
        s5re_o[...] = s5re0_ref[...]
        s5im_o[...] = s5im0_ref[...]

    x = x_ref[...]
    r8 = lax.broadcasted_iota(jnp.int32, (SUBLANES, 1), 0)
    h = _rmsnorm_mod(x, norm_ref[...], mod_ref[1:2, :], mod_ref[0:1, :])
    proj = _mm(h, w_in_ref[...])
    u_gate = proj[:, 0:256]
    u_lru = proj[:, 256:512]
    p_rw = proj[:, 512:2304]
    u_s5 = proj[:, 2304:2560]

    branch = {}

    def lru_step():
        cw = conv_w_ref[...]
        lru_tail = lru_buf_o[...]
        xc = conv_b_ref[...] + _shift_in(u_lru, LRU_CONV - 1, lru_tail) * cw[0:1]
        for k in range(1, LRU_CONV - 1):
            xc = xc + _shift_in(u_lru, LRU_CONV - 1 - k, lru_tail) * cw[k:k + 1]
        xc = xc + u_lru * cw[LRU_CONV - 1:LRU_CONV]
        lru_buf_o[...] = u_lru[tc - SUBLANES:tc, :]
        r_gate = jax.nn.sigmoid(_mm(xc, wa_ref[...]) + ba_ref[...])
        i_gate = jax.nn.sigmoid(_mm(xc, wx_ref[...]) + bx_ref[...])
        log_a = -LRU_C * r_gate * jax.nn.softplus(-lam_ref[...])
        a = jnp.exp(log_a)
        gain = jnp.sqrt(jnp.maximum(-_expm1(2.0 * log_a), 0.0))
        b = gain * i_gate * xc
        b = jnp.concatenate([b[0:SUBLANES] + jnp.where(r8 == 0, a[0:SUBLANES] * lru_h_o[...], 0.0),
                             b[SUBLANES:]], axis=0)
        h_lru = _linear_scan(a, b)
        lru_h_o[...] = h_lru[tc - 1:tc, :]
        branch["y_a"] = h_lru * jax.nn.gelu(u_gate)

    n_sub = tc // S5_SUB
    u_bf = u_s5.astype(BF16)

    def s5_lag_step():
        lag_shift = lag_shift_ref[...]
        lb = lag_shift.shape[1]
        rows = []
        for r0 in range(0, tc, lb):
            u_blk = u_bf[r0:r0 + lb]
            lagged = jnp.dot(lag_shift, u_blk, preferred_element_type=F32)
            rows.append(jnp.concatenate(
                [u_blk] + [lagged[(d - 1) * lb:d * lb].astype(BF16) for d in range(1, S5_SUB)], axis=1))
        branch["lags"] = jnp.concatenate(rows, axis=0)

    def s5_intra_step():
        branch["y_intra"] = jnp.dot(branch["lags"], lagk_ref[...], preferred_element_type=F32)

    def s5_input_step():
        branch["bu"] = (_mm(u_bf, bbar_re[...]), _mm(u_bf, bbar_im[...]))

    def s5_state_step():
        bu_re, bu_im = branch["bu"]
        te_re = jnp.concatenate([tab_end_re[...]] * n_sub, axis=0)
        te_im = jnp.concatenate([tab_end_im[...]] * n_sub, axis=0)
        sub_sum = sub_sum_ref[...]
        e_re = _mm(sub_sum, (te_re * bu_re - te_im * bu_im).astype(BF16))[0:n_sub]
        e_im = _mm(sub_sum, (te_re * bu_im + te_im * bu_re).astype(BF16))[0:n_sub]
        h0r = s5re_o[...]
        h0i = s5im_o[...]
        k_sub = int(math.log2(S5_SUB))
        ar = pw_re[k_sub:k_sub + 1, :]
        ai = pw_im[k_sub:k_sub + 1, :]
        if n_sub <= SUBLANES:
            cr, ci = h0r, h0i
            prev_r, prev_i = [], []
            for n in range(n_sub):
                prev_r.append(jnp.broadcast_to(cr, (S5_SUB, S5_STATES)))
                prev_i.append(jnp.broadcast_to(ci, (S5_SUB, S5_STATES)))
                cr, ci = ar * cr - ai * ci + e_re[n:n + 1], ar * ci + ai * cr + e_im[n:n + 1]
            s5re_o[...] = cr
            s5im_o[...] = ci
            pvr = jnp.concatenate(prev_r, axis=0)
            pvi = jnp.concatenate(prev_i, axis=0)
        else:
            rn = lax.broadcasted_iota(jnp.int32, (n_sub, 1), 0)
            hr = e_re + jnp.where(rn == 0, ar * h0r - ai * h0i, 0.0)
            hi = e_im + jnp.where(rn == 0, ar * h0i + ai * h0r, 0.0)
            d = 1
            while d < n_sub:
                k = k_sub + int(math.log2(d))
                pr = pw_re[k:k + 1, :]
                pi = pw_im[k:k + 1, :]
                sr = _shift_rows(hr, d, 0.0)
                si = _shift_rows(hi, d, 0.0)
                hr, hi = hr + (pr * sr - pi * si), hi + (pr * si + pi * sr)
                d *= 2
            s5re_o[...] = hr[n_sub - 1:n_sub, :]
            s5im_o[...] = hi[n_sub - 1:n_sub, :]
            pvr = jnp.where(rn == 0, h0r, pltpu.roll(hr, 1, axis=0))
            pvi = jnp.where(rn == 0, h0i, pltpu.roll(hi, 1, axis=0))
            pvr = jnp.broadcast_to(pvr[:, None, :], (n_sub, S5_SUB, S5_STATES)).reshape(tc, S5_STATES)
            pvi = jnp.broadcast_to(pvi[:, None, :], (n_sub, S5_SUB, S5_STATES)).reshape(tc, S5_STATES)
        branch["pv"] = (pvr, pvi)

    def s5_out_step():
        pvr, pvi = branch["pv"]
        ti_re = jnp.concatenate([tab_in_re[...]] * n_sub, axis=0)
        ti_im = jnp.concatenate([tab_in_im[...]] * n_sub, axis=0)
        y_inter = (_mm(ti_re * pvr - ti_im * pvi, cre_ref[...])
                   - _mm(ti_re * pvi + ti_im * pvr, cim_ref[...]))
        branch["y5"] = branch["y_intra"] + y_inter + d_ref[...] * u_s5

    def s5_glu_step():
        z = jax.nn.gelu(branch["y5"])
        branch["y_c"] = z * jax.nn.sigmoid(_mm(z, gluw_ref[...]) + glub_ref[...])

    fillers = iter([s5_lag_step, lru_step, s5_intra_step, s5_input_step, s5_state_step, s5_out_step,
                    s5_glu_step])

    def fill():
        step = next(fillers, None)
        if step is not None:
            step()

    rolled = pltpu.roll(p_rw, 1, axis=0)
    prev =jnp.concatenate([jnp.where(r8 == 0, shift_o[...], rolled[0:SUBLANES]), rolled[SUBLANES:]], axis=0)
    shift_o[...] = p_rw[tc - 1:tc, :]
    ones_bd = ones_ref[...]

    def head_sum(z):
        w = ones_bd.shape[0]
        return jnp.concatenate([_mm(z[:, j:j + w], ones_bd) for j in range(0, RWKV_WIDTH, w)], axis=1)

    def rwkv_operands(rows):
        xm = p_rw[rows] + (prev[rows] - p_rw[rows]) * mu_ref[...]
        r_t = xm[:, 0:512]
        k_t = xm[:, 512:1024]
        v_t = xm[:, 1024:1536]
        lo = xm[:, 1536:1664]
        g_lo = xm[:, 1664:1792]
        log_w = -jax.nn.softplus(-(w0_ref[...] + _mm(jnp.tanh(lo), w2_ref[...]))) - 0.5
        a_t = jax.nn.sigmoid(a0_ref[...] + _mm(lo, a2_ref[...]))
        g_t = _mm(jax.nn.sigmoid(g_lo), g2_ref[...])
        kk = k_t * kk_ref[...]
        kk = kk * lax.rsqrt(jnp.maximum(head_sum(kk * kk), 1e-24))
        kp = k_t * (1.0 + (a_t - 1.0) * ka_ref[...])
        ld_t = -jnp.exp(log_w)
        return r_t, kp, v_t, kk, kk * a_t, ld_t, g_t, r_t * kp * rk_ref[...]

    gw = group * RWKV_HEAD
    crow = lax.broadcasted_iota(jnp.int32, (chunk, group * chunk), 0)
    ccol = lax.broadcasted_iota(jnp.int32, (chunk, group * chunk), 1) & (chunk - 1)
    strict = ccol < crow
    incl = ccol <= crow
    diag = ccol == crow
    n_pow = int(math.log2(chunk))

    n_chunks = tc // chunk
    n_groups = RWKV_HEADS // group
    lanes = [slice(g * gw, (g + 1) * gw) for g in range(n_groups)]

    a3s, kgbg3, v_c, g_end_c, lv, rb3, t3, gate_c, rkr_c, vrow_c = {}, {}, {}, {}, {}, {}, {}, {}, {}, {}

    def prepare_chunks(c0, c1):
        ops = rwkv_operands(slice(c0 * chunk, c1 * chunk))
        chains = [(c, g) for c in range(c0, c1) for g in range(n_groups)]
        a3, bb3, kb3, v3 = {}, {}, {}, {}
        for c in range(c0, c1):
            sl = slice((c - c0) * chunk, (c - c0 + 1) * chunk)
            r, kpc, v, kkc, bq, ld, gate_c[c], rkr_c[c] = (z[sl] for z in ops)
            vrow_c[c] = v
            _stage_chunk(c, r, kpc, v, kkc, bq, ld, a3, bb3, kb3, v3)
        _solve_chunks(chains, a3, bb3, kb3, v3)

    def _stage_chunk(c, r, kpc, v, kkc, bq, ld, a3, bb3, kb3, v3):
        cum = ld
        d = 1
        while d < chunk:
            cum = cum + _shift_rows(cum, d, 0.0)
            d *= 2
        cum_end = cum[chunk - 1:chunk, :]
        inv = jnp.exp(-cum)
        tail_decay = jnp.exp(cum_end - cum)
        rt = r * jnp.exp(cum)
        kkt = kkc * jnp.exp(cum - ld)
        kb = kpc * inv
        bb = bq * inv
        kg = kpc * tail_decay
        bg = bq * tail_decay
        g_end_c[c] = jnp.exp(cum_end)
        for g, ls in enumerate(lanes):
            a_in = jnp.concatenate([kkt[:, ls], rt[:, ls]], axis=0)
            a3[c, g] = _lhs3(a_in, 1, PASSES_INTRA)
            a3s[c, g] = a3[c, g] if PASSES_STATE == PASSES_INTRA else _lhs3(a_in, 1, PASSES_STATE)
            bb3[c, g] = _rhs3(bb[:, ls], 1, group, PASSES_INTRA)
            kb3[c, g] = _rhs3(kb[:, ls], 1, group, PASSES_INTRA)
            v3[c, g] = _rhs3(v[:, ls], 0, group, PASSES_INTRA)
            kgbg3[c, g] = _rhs3(jnp.concatenate([kg[:, ls], -bg[:, ls]], axis=0), 0, 1, PASSES_STATE)
            v_c[c, g] = v[:, ls]

    def _solve_chunks(chains, a3, bb3, kb3, v3):
        xb = {ch: _dg(a3[ch], bb3[ch], 1, 1) for ch in chains}
        xk = {ch: _dg(a3[ch], kb3[ch], 1, 1) for ch in chains}
        n = {ch: jnp.where(strict, -xb[ch][0:chunk], 0.0) for ch in chains}
        for ch in chains:
            rb3[ch] = _lhs3(jnp.where(incl, xb[ch][chunk:], 0.0), 1, PASSES_OUT)
            lv[ch] = _dg(_lhs3(jnp.concatenate([jnp.where(strict, xk[ch][0:chunk], 0.0),
                                                 jnp.where(incl, xk[ch][chunk:], 0.0)], axis=0), 1, PASSES_INTRA),
                         v3[ch], 1, 0)
        t = {ch: jnp.where(diag, 1.0, n[ch]) for ch in chains}
        p = {ch: _dg(_lhs3(n[ch], 1, PASSES_INV), _rhs3(n[ch], 0, group, PASSES_INV), 1, 0) for ch in chains}
        for q in range(1, n_pow):
            last = q + 1 == n_pow
            passes = PASSES_INV if q <= INV_ACCURATE_ROUNDS else 1
            tp = {ch: _dg(_lhs3(t[ch] if last else jnp.concatenate([t[ch], p[ch]], axis=0), 1, passes),
                          _rhs3(p[ch], 0, group, passes), 1, 0) for ch in chains}
            t = {ch: t[ch] + tp[ch][0:chunk] for ch in chains}
            if not last:
                p = {ch: tp[ch][chunk:] for ch in chains}
        for ch in chains:
            t3[ch] = _lhs3(t[ch], 1, PASSES_STATE)

    prepare_chunks(0, n_chunks)

    inv_n = 1.0 / RWKV_HEAD

    def chunk_out(c, y):
        yc = y - head_sum(y) * inv_n
        var = head_sum(yc * yc) * inv_n
        yn = yc * lax.rsqrt(var + RWKV_GN_EPS) * lnw_ref[...] + lnb_ref[...]
        return (yn + head_sum(rkr_c[c]) * vrow_c[c]) * gate_c[c]

    yb_rows = []
    y_prev = None
    for c in range(n_chunks):
        s_prev = [s_o[:, ls] for ls in lanes]
        xs = [_dg(a3s[c, g], _rhs3(s_prev[g], 1, group, PASSES_STATE), 1, 1) for g in range(n_groups)]
        if y_prev is not None:
            yb_rows.append(chunk_out(c - 1, y_prev))
        fill()
        u = [_dg(t3[c, g], _rhs3(xs[g][0:chunk] + lv[c, g][0:chunk], 0, group, PASSES_STATE), 1, 0)
             for g in range(n_groups)]
        fill()
        ru = [_dg(rb3[c, g], _rhs3(u[g], 0, group, PASSES_OUT), 1, 0) for g in range(n_groups)]
        upd = [_dg(_lhs3(jnp.concatenate([v_c[c, g], u[g]], axis=0), 0, PASSES_STATE), kgbg3[c, g], 0, 0)
               for g in range(n_groups)]
        y_prev = jnp.concatenate(
            [xs[g][chunk:] + lv[c, g][chunk:] - ru[g] for g in range(n_groups)], axis=1)
        for g, ls in enumerate(lanes):
            s_o[:, ls] = s_prev[g] * g_end_c[c][:, ls] + _fold_diag(upd[g], group)
    yb_rows.append(chunk_out(n_chunks - 1, y_prev))
    y_b = jnp.concatenate(yb_rows, axis=0)

    for _ in range(7):
        fill()
    mix = _mm(jnp.concatenate([branch["y_a"], y_b, branch["y_c"]], axis=-1), w_out_ref[...])
    xo_ref[...] = x + mod_ref[2:3, :] * mix


def _lag_shift_matrix(tc):
    t = jnp.arange(tc)
    blocks = [((t[:, None] - d == t[None, :]) & ((t % S5_SUB) >= d)[:, None]) for d in range(1, S5_SUB)]
    return jnp.concatenate(blocks, axis=0).astype(BF16)


def _const_spec(arr):
    nd = arr.ndim
    return pl.BlockSpec(arr.shape, lambda b, i: (0,) * nd)


def _batch_spec(arr):
    nd = arr.ndim
    return pl.BlockSpec((None,) + arr.shape[1:], lambda b, i: (b,) + (0,) * (nd - 1))


def _mixer(tl, x, mod, lp, st):
    bsz, seq, _ = x.shape
    tc = tl.t_mix
    consts1 = [lp["norm_mix"], lp["w_in"], lp["w_out"],
               lp["lru_conv_w"], lp["lru_conv_b"], lp["lru_wa"], lp["lru_ba"], lp["lru_wx"],
               lp["lru_bx"], lp["lru_lambda"]]
    states1 = [st["lru_buf"], st["lru_h"]]
    consts2 = [lp["rwkv_mu"], lp["rwkv_w0"], lp["rwkv_w2"], lp["rwkv_a0"], lp["rwkv_a2"],
               lp["rwkv_g2"], lp["rwkv_k_k"], lp["rwkv_k_a"], lp["rwkv_r_k"], lp["rwkv_ln_w"],
               lp["rwkv_ln_b"], lp["ones_bd"]]
    states2 = [st["rwkv_shift"], st["rwkv_s"]]
    sub_rows = max(tc // S5_SUB, SUBLANES)
    sub_sum = (jnp.arange(tc)[None, :] // S5_SUB == jnp.arange(sub_rows)[:, None]).astype(BF16)
    consts3 = list(lp["s5_tables"]) + [_lag_shift_matrix(min(tc, 256)), sub_sum, lp["s5_c_re"], lp["s5_c_im"],
                                       lp["s5_d"], lp["s5_glu_w"], lp["s5_glu_b"]]
    states3 = [st["s5_re"], st["s5_im"]]
    x_spec = pl.BlockSpec((None, tc, D_MODEL), lambda b, i: (b, i, 0))
    in_specs = ([x_spec, _batch_spec(mod)]
                + [_const_spec(a) for a in consts1] + [_batch_spec(a) for a in states1]
                + [_const_spec(a) for a in consts2] + [_batch_spec(a) for a in states2]
                + [_const_spec(a) for a in consts3] + [_batch_spec(a) for a in states3])
    state_list = states1 + states2 + states3
    out_shape = [jax.ShapeDtypeStruct(x.shape, F32)] + [jax.ShapeDtypeStruct(a.shape, F32) for a in state_list]
    out_specs = [x_spec] + [_batch_spec(a) for a in state_list]
    outs = pl.pallas_call(
        functools.partial(_mixer_kernel, tl),
        grid=(bsz, seq // tc),
        in_specs=in_specs, out_specs=out_specs, out_shape=out_shape,
        compiler_params=pltpu.CompilerParams(
            dimension_semantics=("arbitrary", "arbitrary"), vmem_limit_bytes=VMEM_LIMIT_BYTES),
        name="mixer",
    )(x, mod, *consts1, *states1, *consts2, *states2, *consts3, *states3)
    x_new = outs[0]
    keys = ["lru_buf", "lru_h", "rwkv_shift", "rwkv_s", "s5_re", "s5_im"]
    return x_new, dict(zip(keys, outs[1:]))


def _ffn_kernel(tl, final,
                x_ref, mod_ref, norm_ref, up_ref, cw_ref, cb_ref, down_ref, tail0_ref, fin_ref,
                xo_ref, tail_o):
    tf = tl.t_ffn
    i = pl.program_id(1)

    @pl.when(i == 0)
    def _init():
        tail_o[...] = tail0_ref[...]

    x = x_ref[...]
    h = _rmsnorm_mod(x, norm_ref[...], mod_ref[4:5, :], mod_ref[3:4, :]).astype(BF16)

    def up_block(c0):
        return jnp.dot(h, up_ref[:, c0:c0 + FFN_COL_BLOCK], preferred_element_type=F32)

    def conv_block(up, c0):
        cs = slice(c0, c0 + FFN_COL_BLOCK)
        tail = tail_o[:, cs]
        m1 = _shift_in(up, 1, tail)
        m2 = _shift_in(up, 2, tail)
        tail_o[:, cs] = up[tf - SUBLANES:tf, :]
        return cb_ref[:, cs] + m2 * cw_ref[0:1, cs] + m1 * cw_ref[1:2, cs] + up * cw_ref[2:3, cs]

    blocks = list(range(0, D_FF, FFN_COL_BLOCK))
    pending = [(up_block(c), up_block(D_FF + c)) for c in blocks[:FFN_LOOKAHEAD]]
    acc = None
    acts = []
    for j, c0 in enumerate(blocks):
        up_val, up_gate = pending.pop(0)
        if j + FFN_LOOKAHEAD < len(blocks):
            nxt = blocks[j + FFN_LOOKAHEAD]
            pending.append((up_block(nxt), up_block(D_FF + nxt)))
        val = conv_block(up_val, c0)
        gate = conv_block(up_gate, D_FF + c0)
        acts.append((val * (gate * jax.nn.sigmoid(gate))).astype(BF16))
        if len(acts) == FFN_DOWN_GROUP or j + 1 == len(blocks):
            r0 = c0 + FFN_COL_BLOCK - len(acts) * FFN_COL_BLOCK
            part = jnp.dot(jnp.concatenate(acts, axis=1), down_ref[r0:c0 + FFN_COL_BLOCK, :],
                           preferred_element_type=F32)
            acc = part if acc is None else acc + part
            acts = []
    y = x + mod_ref[5:6, :] * acc
    if final:
        y = y * lax.rsqrt(jnp.mean(y * y, axis=-1, keepdims=True) + NORM_EPS) * fin_ref[...]
    xo_ref[...] = y


def _ffn(tl, final, x, mod, lp, tail0, norm_final):
    bsz, seq, _ = x.shape
    tf = tl.t_ffn
    consts = [lp["norm_ffn"], lp["ffn_up"], lp["ffn_conv_w"], lp["ffn_conv_b"], lp["ffn_down"]]
    x_spec = pl.BlockSpec((None, tf, D_MODEL), lambda b, i: (b, i, 0))
    in_specs = ([x_spec, _batch_spec(mod)] + [_const_spec(a) for a in consts]
                + [_batch_spec(tail0), _const_spec(norm_final)])
    x_new, tail = pl.pallas_call(
        functools.partial(_ffn_kernel, tl, final),
        grid=(bsz, seq // tf),
        in_specs=in_specs,
        out_specs=[x_spec, _batch_spec(tail0)],
        out_shape=[jax.ShapeDtypeStruct(x.shape, F32), jax.ShapeDtypeStruct(tail0.shape, F32)],
        compiler_params=pltpu.CompilerParams(
            dimension_semantics=("arbitrary", "arbitrary"), vmem_limit_bytes=VMEM_LIMIT_BYTES),
        name="conv_ffn",
    )(x, mod, *consts, tail0, norm_final)
    return x_new, tail


def _block_diag_dense(blocks):
    n, r, c = blocks.shape
    eye = jnp.eye(n, dtype=blocks.dtype)
    return (eye[:, None, :, None] * blocks[:, :, None, :]).reshape(n * r, n * c)


def _layer_params(p, l):
    row = lambda a: a.reshape(1, -1)
    zeros_lora = jnp.zeros((64, RWKV_WIDTH), F32)
    head = jnp.arange(RWKV_WIDTH) // RWKV_HEAD
    lp = {
        "norm_mix": row(p["norm_mix"][l]), "norm_ffn": row(p["norm_ffn"][l]),
        "w_in": p["w_in"][l].astype(BF16), "w_out": p["w_out"][l].astype(BF16),
        "lru_conv_w": p["lru_conv_w"][l], "lru_conv_b": row(p["lru_conv_b"][l]),
        "lru_wa": _block_diag_dense(p["lru_wa"][l]).astype(BF16), "lru_ba": row(p["lru_ba"][l]),
        "lru_wx": _block_diag_dense(p["lru_wx"][l]).astype(BF16), "lru_bx": row(p["lru_bx"][l]),
        "lru_lambda": row(p["lru_lambda"][l]),
        "rwkv_mu": row(p["rwkv_mu"][l]), "rwkv_w0": row(p["rwkv_w0"][l]),
        "rwkv_w2": jnp.concatenate([p["rwkv_w2"][l], zeros_lora], axis=0).astype(BF16),
        "rwkv_a0": row(p["rwkv_a0"][l]),
        "rwkv_a2": jnp.concatenate([zeros_lora, p["rwkv_a2"][l]], axis=0).astype(BF16),
        "rwkv_g2": p["rwkv_g2"][l].astype(BF16),
        "rwkv_k_k": row(p["rwkv_k_k"][l]), "rwkv_k_a": row(p["rwkv_k_a"][l]),
        "rwkv_r_k": row(p["rwkv_r_k"][l]), "rwkv_ln_w": row(p["rwkv_ln_w"][l]),
        "rwkv_ln_b": row(p["rwkv_ln_b"][l]),
        "ones_bd": (head[:256, None] == head[None, :256]).astype(BF16),
        "s5_a_re": row(p["s5_a_re"][l]), "s5_a_im": row(p["s5_a_im"][l]),
        "s5_log_dt": row(jnp.repeat(p["s5_log_dt"][l], 64)),
        "s5_b_re": _block_diag_dense(jnp.swapaxes(p["s5_b_re"][l], 1, 2)),
        "s5_b_im": _block_diag_dense(jnp.swapaxes(p["s5_b_im"][l], 1, 2)),
        "s5_c_re_f32": _block_diag_dense(jnp.swapaxes(p["s5_c_re"][l], 1, 2)),
        "s5_c_im_f32": _block_diag_dense(jnp.swapaxes(p["s5_c_im"][l], 1, 2)),
        "s5_d": row(p["s5_d"][l]),
        "s5_glu_w": p["s5_glu_w"][l].astype(BF16), "s5_glu_b": row(p["s5_glu_b"][l]),
        "ffn_up": p["ffn_up"][l].astype(BF16), "ffn_conv_w": p["ffn_conv_w"][l],
        "ffn_conv_b": row(p["ffn_conv_b"][l]), "ffn_down": p["ffn_down"][l].astype(BF16),
    }
    lp["s5_tables"] = _s5_prep(lp)
    lp["s5_c_re"] = lp["s5_c_re_f32"].astype(BF16)
    lp["s5_c_im"] = lp["s5_c_im_f32"].astype(BF16)
    return lp


def _pad_tail(buf):
    return jnp.pad(buf, ((0, 0), (SUBLANES - buf.shape[1], 0), (0, 0)))


def _state_in(states, l):
    lru_buf, lru_h, rw_shift, rw_s, s5_re, s5_im, ffn_buf = (s[l] for s in states)
    bsz = lru_h.shape[0]
    return {
        "lru_buf": _pad_tail(lru_buf),
        "lru_h": lru_h.reshape(bsz, 1, LRU_WIDTH),
        "rwkv_shift": rw_shift.reshape(bsz, 1, RWKV_COLS),
        "rwkv_s": jnp.transpose(rw_s, (0, 2, 1, 3)).reshape(bsz, RWKV_HEAD, RWKV_WIDTH),
        "s5_re": s5_re.reshape(bsz, 1, S5_STATES),
        "s5_im": s5_im.reshape(bsz, 1, S5_STATES),
    }, _pad_tail(ffn_buf)


def _state_out(st, ffn_tail):
    bsz = st["lru_h"].shape[0]
    return (st["lru_buf"][:, SUBLANES - (LRU_CONV - 1):, :],
            st["lru_h"].reshape(bsz, LRU_WIDTH),
            st["rwkv_shift"].reshape(bsz, RWKV_COLS),
            jnp.transpose(st["rwkv_s"].reshape(bsz, RWKV_HEAD, RWKV_HEADS, RWKV_HEAD), (0, 2, 1, 3)),
            st["s5_re"].reshape(bsz, 16, 64),
            st["s5_im"].reshape(bsz, 16, 64),
            ffn_tail[:, SUBLANES - 2:, :])


def _trunk(x, mods, states, layer_params, norm_final):
    depth = len(layer_params)
    tl = _tiling(x.shape[1])
    new = []
    for l in range(depth):
        st, ffn_tail0 = _state_in(states, l)
        x, st = _mixer(tl, x, mods[l], layer_params[l], st)
        x, ffn_tail = _ffn(tl, l == depth - 1, x, mods[l], layer_params[l], ffn_tail0, norm_final)
        new.append(_state_out(st, ffn_tail))
    stacked = tuple(jnp.stack([n[j] for n in new], axis=0) for j in range(7))
    return x, stacked


def _zero_states(depth, bsz):
    return (jnp.zeros((depth, bsz, LRU_CONV - 1, LRU_WIDTH), F32),
            jnp.zeros((depth, bsz, LRU_WIDTH), F32),
            jnp.zeros((depth, bsz, RWKV_COLS), F32),
            jnp.zeros((depth, bsz, RWKV_HEADS, RWKV_HEAD, RWKV_HEAD), F32),
            jnp.zeros((depth, bsz, 16, 64), F32),
            jnp.zeros((depth, bsz, 16, 64), F32),
            jnp.zeros((depth, bsz, 2, 2 * D_FF), F32))


def kernel(x_prompt, x_sample, c_prompt, c_sample, state_lru_conv, state_lru_h, state_rwkv_shift, state_rwkv_S, state_s5_re, state_s5_im, state_ffn_conv, w_ada, b_ada, norm_mix, norm_ffn, w_in, w_out, lru_conv_w, lru_conv_b, lru_wa, lru_ba, lru_wx, lru_bx, lru_lambda, rwkv_mu, rwkv_w0, rwkv_w2, rwkv_a0, rwkv_a2, rwkv_g2, rwkv_k_k, rwkv_k_a, rwkv_r_k, rwkv_ln_w, rwkv_ln_b, s5_a_re, s5_a_im, s5_b_re, s5_b_im, s5_c_re, s5_c_im, s5_d, s5_log_dt, s5_glu_w, s5_glu_b, ffn_up, ffn_conv_w, ffn_conv_b, ffn_down, norm_final):
    p = dict(norm_mix=norm_mix, norm_ffn=norm_ffn, w_in=w_in, w_out=w_out,
             lru_conv_w=lru_conv_w, lru_conv_b=lru_conv_b, lru_wa=lru_wa, lru_ba=lru_ba,
             lru_wx=lru_wx, lru_bx=lru_bx, lru_lambda=lru_lambda,
             rwkv_mu=rwkv_mu, rwkv_w0=rwkv_w0, rwkv_w2=rwkv_w2, rwkv_a0=rwkv_a0, rwkv_a2=rwkv_a2,
             rwkv_g2=rwkv_g2, rwkv_k_k=rwkv_k_k, rwkv_k_a=rwkv_k_a,
             rwkv_r_k=rwkv_r_k.reshape(rwkv_r_k.shape[0], RWKV_WIDTH),
             rwkv_ln_w=rwkv_ln_w, rwkv_ln_b=rwkv_ln_b,
             s5_a_re=s5_a_re, s5_a_im=s5_a_im, s5_b_re=s5_b_re, s5_b_im=s5_b_im,
             s5_c_re=s5_c_re, s5_c_im=s5_c_im, s5_d=s5_d, s5_log_dt=s5_log_dt,
             s5_glu_w=s5_glu_w, s5_glu_b=s5_glu_b,
             ffn_up=ffn_up, ffn_conv_w=ffn_conv_w, ffn_conv_b=ffn_conv_b, ffn_down=ffn_down)
    depth = w_in.shape[0]
    layer_params = [_layer_params(p, l) for l in range(depth)]
    fin = norm_final.reshape(1, D_MODEL)
    n_prompt = c_prompt.shape[0]
    mod = _ada(jnp.concatenate([c_prompt, c_sample], axis=0), w_ada, b_ada)
    mod = mod.reshape(depth, mod.shape[1], 6, D_MODEL)
    mods_p = [mod[l, :n_prompt] for l in range(depth)]
    mods_s = [mod[l, n_prompt:] for l in range(depth)]
    s_in = (state_lru_conv, state_lru_h, state_rwkv_shift, state_rwkv_S, state_s5_re, state_s5_im,
            state_ffn_conv)
    y_sample, s_states = _trunk(x_sample, mods_s, s_in, layer_params, fin)
    y_prompt, p_states = _trunk(x_prompt, mods_p, _zero_states(depth, x_prompt.shape[0]), layer_params, fin)
    return (y_prompt, y_sample) + tuple(p_states) + tuple(s_states)
```
